```python
import math
import jax, jax.numpy as jnp
from jax import lax
import numpy as np

D_MODEL = 2048
BATCH = 4
SEQ = 2048
DEPTH = 2
DEC_BATCH = 8
DEC_SEQ = 1
PAST_LEN = 16384
PAGE_SIZE = 128

HEAD_DIM = 128
H_FOX = 8
H_SB = 8
H_DIFF = 8
DIFF_DIM = 128
D_FF = 4 * D_MODEL
Q_BLOCK = 128
N_EVEN = (DEPTH + 1) // 2
N_ODD = DEPTH // 2
ALPHA = (2 * DEPTH) ** 0.25
BETA = (8 * DEPTH) ** -0.25
LN_EPS = 1e-5
NEG = -1e30
W_FOX = H_FOX * HEAD_DIM
W_SB = H_SB * HEAD_DIM
IN_EVEN = 3 * W_FOX + H_FOX + 3 * W_SB
SPLIT_EVEN = [W_FOX, 2 * W_FOX, 3 * W_FOX, 3 * W_FOX + H_FOX, 3 * W_FOX + H_FOX + W_SB, 3 * W_FOX + H_FOX + 2 * W_SB]
W_DIFF_QK = 2 * H_DIFF * DIFF_DIM
W_DIFF_V = H_DIFF * 2 * DIFF_DIM
IN_ODD = 2 * W_DIFF_QK + W_DIFF_V

kernel_name = 'fox_stickbreak_diffattn_deepnorm_step'


def layer_norm(x, g, b):
    xf = x.astype(jnp.float32)
    mu = jnp.mean(xf, axis=-1, keepdims=True)
    var = jnp.mean(jnp.square(xf - mu), axis=-1, keepdims=True)
    return ((xf - mu) * lax.rsqrt(var + LN_EPS) * g + b).astype(x.dtype)


def rms_norm(x, g):
    xf = x.astype(jnp.float32)
    return (xf * lax.rsqrt(jnp.mean(jnp.square(xf), axis=-1, keepdims=True) + LN_EPS) * g).astype(x.dtype)


def sq_relu_mlp(x, w_up, w_down):
    return jnp.square(jax.nn.relu(x @ w_up)) @ w_down


def gather_pages(cache, page_table):
    g = cache[page_table]
    return g.reshape(g.shape[0], g.shape[1] * g.shape[2], *g.shape[3:])


def to_blocks(a):
    b, t = a.shape[:2]
    return jnp.swapaxes(a.reshape(b, t // Q_BLOCK, Q_BLOCK, *a.shape[2:]), 0, 1)


def from_blocks(a):
    a = jnp.swapaxes(a, 0, 1)
    return a.reshape(a.shape[0], a.shape[1] * a.shape[2], *a.shape[3:])


def attend(core, qa, qpos, ka, kpos):
    t = qpos.shape[0]
    if t <= Q_BLOCK or t % Q_BLOCK != 0:
        return core(qa, qpos, ka, kpos)
    qb = tuple(to_blocks(a) for a in qa)
    pb = qpos.reshape(t // Q_BLOCK, Q_BLOCK)
    out = lax.map(lambda xs: core(xs[0], xs[1], ka, kpos), (qb, pb))
    return from_blocks(out)


def fox_core(qa, qpos, ka, kpos):
    q, cq = qa
    k, v, ck = ka
    s = jnp.einsum('bqhd,bkhd->bhqk', q, k).astype(jnp.float32) * (HEAD_DIM ** -0.5)
    s = s + jnp.swapaxes(ck, 1, 2)[:, :, None, :] - jnp.swapaxes(cq, 1, 2)[:, :, :, None]
    causal = kpos[None, :] <= qpos[:, None]
    p = jax.nn.softmax(jnp.where(causal, s, NEG), axis=-1)
    return jnp.einsum('bhqk,bkhd->bqhd', p.astype(v.dtype), v)


def sb_core(qa, qpos, ka, kpos):
    (q,) = qa
    k, v = ka
    z = jnp.einsum('bqhd,bkhd->bhqk', q, k).astype(jnp.float32) * (HEAD_DIM ** -0.5)
    strict = kpos[None, :] < qpos[:, None]
    log_keep = jnp.where(strict, jax.nn.log_sigmoid(-z), 0.0)
    tail = lax.cumsum(log_keep, axis=3, reverse=True) - log_keep
    a = jnp.where(strict, jnp.exp(jax.nn.log_sigmoid(z) + tail), 0.0)
    return jnp.einsum('bhqk,bkhd->bqhd', a.astype(v.dtype), v)


def diff_core(qa, qpos, ka, kpos):
    q1, q2 = qa
    k1, k2, v, lam = ka
    slopes = 2.0 ** (-8.0 * jnp.arange(1, H_DIFF + 1, dtype=jnp.float32) / H_DIFF)
    dist = (qpos[:, None] - kpos[None, :]).astype(jnp.float32)
    bias = -slopes[:, None, None] * dist
    causal = kpos[None, :] <= qpos[:, None]
    scale = DIFF_DIM ** -0.5
    s1 = jnp.einsum('bqhd,bkhd->bhqk', q1, k1).astype(jnp.float32) * scale + bias
    s2 = jnp.einsum('bqhd,bkhd->bhqk', q2, k2).astype(jnp.float32) * scale + bias
    p = jax.nn.softmax(jnp.where(causal, s1, NEG), axis=-1) - lam * jax.nn.softmax(jnp.where(causal, s2, NEG), axis=-1)
    return jnp.einsum('bhqk,bkhd->bqhd', p.astype(v.dtype), v)


def even_mixer(x, past, w_in, b_f, w_o, qpos, kpos):
    b, t, _ = x.shape
    h = x @ w_in
    q_a, k_a, v_a, f_a, q_b, k_b, v_b = jnp.split(h, SPLIT_EVEN, axis=-1)
    q_a, k_a, v_a = (a.reshape(b, t, H_FOX, HEAD_DIM) for a in (q_a, k_a, v_a))
    q_b, k_b, v_b = (a.reshape(b, t, H_SB, HEAD_DIM) for a in (q_b, k_b, v_b))
    lf = jax.nn.log_sigmoid((f_a + b_f).astype(jnp.float32))
    new = (k_a, v_a, lf, k_b, v_b)
    if past is not None:
        k_a, v_a, lf_all, k_b, v_b = tuple(jnp.concatenate([p_, n_.astype(p_.dtype) if p_.dtype != jnp.float32 else n_], axis=1) for p_, n_ in zip(past, new))
    else:
        lf_all = lf
    lf_all = lf_all.astype(jnp.float32)
    cum = lax.cumsum(lf_all, axis=1, reverse=True) - lf_all
    o_a = attend(fox_core, (q_a, cum[:, -t:]), qpos, (k_a, v_a, cum), kpos)
    o_b = attend(sb_core, (q_b,), qpos, (k_b, v_b), kpos)
    y = jnp.concatenate([o_a.reshape(b, t, W_FOX), o_b.reshape(b, t, W_SB)], axis=-1) @ w_o
    return y, new


def odd_mixer(x, past, w_in, lq1, lk1, lq2, lk2, g, w_o, qpos, kpos, lambda_init):
    b, t, _ = x.shape
    h = x @ w_in
    q, k, v = jnp.split(h, [W_DIFF_QK, 2 * W_DIFF_QK], axis=-1)
    q = q.reshape(b, t, H_DIFF, 2, DIFF_DIM)
    k = k.reshape(b, t, H_DIFF, 2 * DIFF_DIM)
    v = v.reshape(b, t, H_DIFF, 2 * DIFF_DIM)
    new = (k, v)
    if past is not None:
        k, v = tuple(jnp.concatenate([p_, n_], axis=1) for p_, n_ in zip(past, new))
    f32 = jnp.float32
    lam = (jnp.exp(jnp.sum(lq1.astype(f32) * lk1.astype(f32))) - jnp.exp(jnp.sum(lq2.astype(f32) * lk2.astype(f32))) + lambda_init)
    o = attend(diff_core, (q[:, :, :, 0], q[:, :, :, 1]), qpos, (k[..., :DIFF_DIM], k[..., DIFF_DIM:], v, lam), kpos)
    o = rms_norm(o, g) * (1.0 - lambda_init)
    return o.reshape(b, t, W_DIFF_V) @ w_o, new


def setup_inputs(seed: int = 0) -> dict:
    key = jax.random.key(seed)
    ks = jax.random.split(key, 32)
    f32 = jnp.float32
    n_pages = PAST_LEN // PAGE_SIZE
    n_pool = (5 * DEC_BATCH * n_pages + 3) // 4

    def nrm(k, shape, scale=1.0):
        return jax.random.normal(k, shape, f32) * scale

    return {
        'x_prompt': nrm(ks[0], (BATCH, SEQ, D_MODEL)),
        'x_sample': nrm(ks[1], (DEC_BATCH, DEC_SEQ, D_MODEL)),
        'cache_fox_k': nrm(ks[2], (N_EVEN, n_pool, PAGE_SIZE, H_FOX, HEAD_DIM)),
        'cache_fox_v': nrm(ks[3], (N_EVEN, n_pool, PAGE_SIZE, H_FOX, HEAD_DIM)),
        'cache_fox_logf': jax.nn.log_sigmoid(nrm(ks[4], (N_EVEN, n_pool, PAGE_SIZE, H_FOX))),
        'cache_sb_k': nrm(ks[5], (N_EVEN, n_pool, PAGE_SIZE, H_SB, HEAD_DIM)),
        'cache_sb_v': nrm(ks[6], (N_EVEN, n_pool, PAGE_SIZE, H_SB, HEAD_DIM)),
        'cache_diff_k': nrm(ks[7], (N_ODD, n_pool, PAGE_SIZE, H_DIFF, 2 * DIFF_DIM)),
        'cache_diff_v': nrm(ks[8], (N_ODD, n_pool, PAGE_SIZE, H_DIFF, 2 * DIFF_DIM)),
        'page_table': jax.random.permutation(ks[9], n_pool)[: DEC_BATCH * n_pages].reshape(DEC_BATCH, n_pages).astype(jnp.int32),
        'w_in_even': nrm(ks[10], (N_EVEN, D_MODEL, IN_EVEN), D_MODEL ** -0.5),
        'b_forget': nrm(ks[11], (N_EVEN, H_FOX), 0.1),
        'w_out_even': nrm(ks[12], (N_EVEN, W_FOX + W_SB, D_MODEL), BETA * (W_FOX + W_SB) ** -0.5),
        'w_in_odd': nrm(ks[13], (N_ODD, D_MODEL, IN_ODD), D_MODEL ** -0.5),
        'lambda_q1': nrm(ks[14], (N_ODD, DIFF_DIM), 0.1),
        'lambda_k1': nrm(ks[15], (N_ODD, DIFF_DIM), 0.1),
        'lambda_q2': nrm(ks[16], (N_ODD, DIFF_DIM), 0.1),
        'lambda_k2': nrm(ks[17], (N_ODD, DIFF_DIM), 0.1),
        'subln_gain': 1.0 + nrm(ks[18], (N_ODD, 2 * DIFF_DIM), 0.02),
        'w_out_odd': nrm(ks[19], (N_ODD, W_DIFF_V, D_MODEL), BETA * W_DIFF_V ** -0.5),
        'ln_mix_gain': 1.0 + nrm(ks[20], (DEPTH, D_MODEL), 0.02),
        'ln_mix_bias': nrm(ks[21], (DEPTH, D_MODEL), 0.02),
        'ln_ffn_gain': 1.0 + nrm(ks[22], (DEPTH, D_MODEL), 0.02),
        'ln_ffn_bias': nrm(ks[23], (DEPTH, D_MODEL), 0.02),
        'w_ffn_up': nrm(ks[24], (DEPTH, D_MODEL, D_FF), D_MODEL ** -0.5),
        'w_ffn_down': nrm(ks[25], (DEPTH, D_FF, D_MODEL), BETA * D_FF ** -0.5),
    }


def reference(x_prompt, x_sample, cache_fox_k, cache_fox_v, cache_fox_logf, cache_sb_k, cache_sb_v, cache_diff_k, cache_diff_v, page_table, w_in_even, b_forget, w_out_even, w_in_odd, lambda_q1, lambda_k1, lambda_q2, lambda_k2, subln_gain, w_out_odd, ln_mix_gain, ln_mix_bias, ln_ffn_gain, ln_ffn_bias, w_ffn_up, w_ffn_down):
    t_p = x_prompt.shape[1]
    t_s = x_sample.shape[1]
    past_len = page_table.shape[1] * PAGE_SIZE
    pos_p = jnp.arange(t_p, dtype=jnp.int32)
    qpos_s = past_len + jnp.arange(t_s, dtype=jnp.int32)
    kpos_s = jnp.arange(past_len + t_s, dtype=jnp.int32)
    xp, xs = x_prompt, x_sample
    even_p, even_s, odd_p, odd_s = [], [], [], []
    for layer in range(DEPTH):
        i = layer // 2
        if layer % 2 == 0:
            w = (w_in_even[i], b_forget[i], w_out_even[i])
            yp, rows = even_mixer(xp, None, *w, pos_p, pos_p)
            even_p.append(rows)
            past = tuple(gather_pages(c[i], page_table) for c in (cache_fox_k, cache_fox_v, cache_fox_logf, cache_sb_k, cache_sb_v))
            ys, rows = even_mixer(xs, past, *w, qpos_s, kpos_s)
            even_s.append(rows)
        else:
            lambda_init = 0.8 - 0.6 * math.exp(-0.3 * layer)
            w = (w_in_odd[i], lambda_q1[i], lambda_k1[i], lambda_q2[i], lambda_k2[i], subln_gain[i], w_out_odd[i])
            yp, rows = odd_mixer(xp, None, *w, pos_p, pos_p, lambda_init)
            odd_p.append(rows)
            past = (gather_pages(cache_diff_k[i], page_table), gather_pages(cache_diff_v[i], page_table))
            ys, rows = odd_mixer(xs, past, *w, qpos_s, kpos_s, lambda_init)
            odd_s.append(rows)
        xp = layer_norm(ALPHA * xp + yp, ln_mix_gain[layer], ln_mix_bias[layer])
        xs = layer_norm(ALPHA * xs + ys, ln_mix_gain[layer], ln_mix_bias[layer])
        xp = layer_norm(ALPHA * xp + sq_relu_mlp(xp, w_ffn_up[layer], w_ffn_down[layer]), ln_ffn_gain[layer], ln_ffn_bias[layer])
        xs = layer_norm(ALPHA * xs + sq_relu_mlp(xs, w_ffn_up[layer], w_ffn_down[layer]), ln_ffn_gain[layer], ln_ffn_bias[layer])

    def stk(rows, j):
        return jnp.stack([r[j] for r in rows])

    return (xp, xs,
            stk(even_p, 0), stk(even_p, 1), stk(even_p, 2), stk(even_p, 3), stk(even_p, 4), stk(odd_p, 0), stk(odd_p, 1),
            stk(even_s, 0), stk(even_s, 1), stk(even_s, 2), stk(even_s, 3), stk(even_s, 4), stk(odd_s, 0), stk(odd_s, 1))
```

```python
import functools
import math

import jax
import jax.numpy as jnp
from jax import lax
from jax.experimental import pallas as pl
from jax.experimental.pallas import tpu as pltpu

F32 = jnp.float32
BF16 = jnp.bfloat16
LN_EPS = 1e-5
NEG = -1e30
LANES = 128
CUM_CHUNK = 256
VMEM_LIMIT = 56 * 1024 * 1024


def _cparams(sem):
    return pltpu.CompilerParams(dimension_semantics=sem, vmem_limit_bytes=VMEM_LIMIT)


def _tile(n, pref):
    t = min(n, pref)
    while n % t:
        t //= 2
    return t


def _dot(a, b):
    return jnp.dot(a, b, preferred_element_type=F32)


def _dot_nt(a, b):
    return lax.dot_general(a, b, (((1,), (1,)), ((), ())), preferred_element_type=F32)


def _split_dot(x, u, terms):
    out = None
    r = x
    for t in range(terms):
        h = r.astype(BF16)
        d = _dot(h, u)
        out = d if out is None else out + d
        if t + 1 < terms:
            r = r - h.astype(F32)
    return out


def _suffix_matrix(n):
    r = lax.broadcasted_iota(jnp.int32, (n, n), 0)
    c = lax.broadcasted_iota(jnp.int32, (n, n), 1)
    return jnp.where(r > c, 1.0, 0.0).astype(BF16)


def _log_sigmoid(x):
    return jnp.minimum(x, 0.0) - jnp.log1p(jnp.exp(-jnp.abs(x)))


def _layer_norm(x, g, b):
    mu = jnp.mean(x, axis=-1, keepdims=True)
    xc = x - mu
    var = jnp.mean(xc * xc, axis=-1, keepdims=True)
    return xc * lax.rsqrt(var + LN_EPS) * g + b


def _mm_kernel(x_ref, w_ref, *o_refs):
    acc = _dot(x_ref[...], w_ref[...])
    for o_ref in o_refs:
        o_ref[...] = acc.astype(o_ref.dtype)


def _mm(x, w, out_dtypes, tm=512, tn=1024):
    m, k = x.shape
    n = w.shape[1]
    tm, tn = _tile(m, tm), _tile(n, tn)
    outs = pl.pallas_call(
        _mm_kernel,
        grid=(n // tn, m // tm),
        in_specs=[pl.BlockSpec((tm, k), lambda j, i: (i, 0)),
                  pl.BlockSpec((k, tn), lambda j, i: (0, j))],
        out_specs=[pl.BlockSpec((tm, tn), lambda j, i: (i, j)) for _ in out_dtypes],
        out_shape=[jax.ShapeDtypeStruct((m, n), d) for d in out_dtypes],
        compiler_params=_cparams(("parallel", "parallel")),
        name="proj",
    )(x, w)
    return outs


def _gate_kernel(x_ref, w_ref, b_ref, o_ref):
    o_ref[...] = _log_sigmoid(_dot(x_ref[...], w_ref[...]) + b_ref[...])


def _forget_gate(x, w, b, tm=512):
    m, k = x.shape
    n = w.shape[1]
    tm = _tile(m, tm)
    return pl.pallas_call(
        _gate_kernel,
        grid=(m // tm,),
        in_specs=[pl.BlockSpec((tm, k), lambda i: (i, 0)),
                  pl.BlockSpec((k, n), lambda i: (0, 0)),
                  pl.BlockSpec((1, n), lambda i: (0, 0))],
        out_specs=pl.BlockSpec((tm, n), lambda i: (i, 0)),
        out_shape=jax.ShapeDtypeStruct((m, n), F32),
        compiler_params=_cparams(("parallel",)),
        name="forget_gate",
    )(x, w, b)


def _revcum_kernel(x_ref, o_ref):
    rows, t = x_ref.shape
    ch = min(CUM_CHUNK, t)
    u = _suffix_matrix(ch)
    carry = jnp.zeros((rows, 1), F32)
    for c in reversed(range(t // ch)):
        xc = x_ref[:, c * ch:(c + 1) * ch]
        o_ref[:, c * ch:(c + 1) * ch] = _split_dot(xc, u, 3) + carry
        carry = carry + jnp.sum(xc, axis=1, keepdims=True)


def _suffix_sums(x):
    return pl.pallas_call(
        _revcum_kernel,
        out_shape=jax.ShapeDtypeStruct(x.shape, F32),
        name="suffix_sums",
    )(x)


def _proj_ln_kernel(*refs, n_pairs, alpha):
    a_refs = refs[:n_pairs]
    w_refs = refs[n_pairs:2 * n_pairs]
    x_ref, g_ref, b_ref, y_ref, yb_ref = refs[2 * n_pairs:]
    acc = alpha * x_ref[...]
    for a_ref, w_ref in zip(a_refs, w_refs):
        acc = acc + _dot(a_ref[...], w_ref[...])
    y = _layer_norm(acc, g_ref[...], b_ref[...])
    y_ref[...] = y
    yb_ref[...] = y.astype(BF16)


def _proj_residual_ln(a_list, w_list, x, g, b, alpha, tm=256):
    m, d = x.shape
    tm = _tile(m, tm)
    n_pairs = len(a_list)
    in_specs = ([pl.BlockSpec((tm, a.shape[1]), lambda i: (i, 0)) for a in a_list]
                + [pl.BlockSpec(w.shape, lambda i: (0, 0)) for w in w_list]
                + [pl.BlockSpec((tm, d), lambda i: (i, 0)),
                   pl.BlockSpec((1, d), lambda i: (0, 0)),
                   pl.BlockSpec((1, d), lambda i: (0, 0))])
    return pl.pallas_call(
        functools.partial(_proj_ln_kernel, n_pairs=n_pairs, alpha=alpha),
        grid=(m // tm,),
        in_specs=in_specs,
        out_specs=[pl.BlockSpec((tm, d), lambda i: (i, 0)), pl.BlockSpec((tm, d), lambda i: (i, 0))],
        out_shape=[jax.ShapeDtypeStruct((m, d), F32), jax.ShapeDtypeStruct((m, d), BF16)],
        compiler_params=_cparams(("parallel",)),
        name="out_proj_ln",
    )(*a_list, *w_list, x, g, b)


def _ffn_kernel(xb_ref, x_ref, wu_ref, wd_ref, g_ref, b_ref, y_ref, yb_ref, acc_ref, *, alpha):
    f = pl.program_id(1)

    @pl.when(f == 0)
    def _():
        acc_ref[...] = alpha * x_ref[...]

    h = jnp.maximum(_dot(xb_ref[...], wu_ref[...]), 0.0)
    acc_ref[...] += _dot((h * h).astype(BF16), wd_ref[...])

    @pl.when(f == pl.num_programs(1) - 1)
    def _():
        y = _layer_norm(acc_ref[...], g_ref[...], b_ref[...])
        y_ref[...] = y
        yb_ref[...] = y.astype(BF16)


def _ffn_residual_ln(xb, x, w_up, w_down, g, b, alpha, tm=512, tf=512):
    m, d = x.shape
    dff = w_up.shape[1]
    tm, tf = _tile(m, tm), _tile(dff, tf)
    return pl.pallas_call(
        functools.partial(_ffn_kernel, alpha=alpha),
        grid=(m // tm, dff // tf),
        in_specs=[pl.BlockSpec((tm, d), lambda i, f: (i, 0)),
                  pl.BlockSpec((tm, d), lambda i, f: (i, 0)),
                  pl.BlockSpec((d, tf), lambda i, f: (0, f)),
                  pl.BlockSpec((tf, d), lambda i, f: (f, 0)),
                  pl.BlockSpec((1, d), lambda i, f: (0, 0)),
                  pl.BlockSpec((1, d), lambda i, f: (0, 0))],
        out_specs=[pl.BlockSpec((tm, d), lambda i, f: (i, 0)), pl.BlockSpec((tm, d), lambda i, f: (i, 0))],
        out_shape=[jax.ShapeDtypeStruct((m, d), F32), jax.ShapeDtypeStruct((m, d), BF16)],
        scratch_shapes=[pltpu.VMEM((tm, d), F32)],
        compiler_params=_cparams(("parallel", "arbitrary")),
        name="ffn_ln",
    )(xb, x, w_up, w_down, g, b)


def _pair_tables(n_blocks, descending):
    qi, kj = [], []
    for q in range(n_blocks):
        ks = range(q, -1, -1) if descending else range(q + 1)
        for k in ks:
            qi.append(q)
            kj.append(k)
    return jnp.asarray(qi, jnp.int32), jnp.asarray(kj, jnp.int32)


def _tri_iota(t):
    r = lax.broadcasted_iota(jnp.int32, (t, t), 0)
    c = lax.broadcasted_iota(jnp.int32, (t, t), 1)
    return r, c


def _fox_kernel(qi_tab, kj_tab, q_ref, k_ref, v_ref, ck_ref, cqrow_ref, o_ref,
                m_s, l_s, acc_s, cq_s, *, scale):
    p = pl.program_id(2)
    qi, kj = qi_tab[p], kj_tab[p]
    t = q_ref.shape[0]

    @pl.when(kj == 0)
    def _():
        m_s[...] = jnp.full(m_s.shape, -jnp.inf, F32)
        l_s[...] = jnp.zeros(l_s.shape, F32)
        acc_s[...] = jnp.zeros(acc_s.shape, F32)
        r, c = _tri_iota(t)
        cq_s[...] = jnp.sum(jnp.where(r == c, cqrow_ref[...], 0.0), axis=1, keepdims=True)

    def step(diagonal):
        s = _dot_nt(q_ref[...], k_ref[...]) * scale
        s = s + ck_ref[...] - cq_s[...]
        if diagonal:
            r, c = _tri_iota(t)
            s = jnp.where(c <= r, s, NEG)
        m_old = m_s[...]
        m_new = jnp.maximum(m_old, jnp.max(s, axis=1, keepdims=True))
        alpha = jnp.exp(m_old - m_new)
        pr = jnp.exp(s - m_new)
        l_s[...] = alpha * l_s[...] + jnp.sum(pr, axis=1, keepdims=True)
        acc_s[...] = alpha * acc_s[...] + _dot(pr.astype(BF16), v_ref[...])
        m_s[...] = m_new

    @pl.when(kj < qi)
    def _():
        step(False)

    @pl.when(kj == qi)
    def _():
        step(True)
        o_ref[...] = (acc_s[...] / l_s[...]).astype(o_ref.dtype)


def _fox_attention(q, k, v, cum, batch, heads, dh, tb=512):
    m = q.shape[0]
    t = m // batch
    tb = _tile(t, tb)
    nb = t // tb
    qi_tab, kj_tab = _pair_tables(nb, descending=False)
    grid_spec = pltpu.PrefetchScalarGridSpec(
        num_scalar_prefetch=2,
        grid=(batch, heads, int(qi_tab.shape[0])),
        in_specs=[pl.BlockSpec((tb, dh), lambda b, h, p, qt, kt: (b * nb + qt[p], h)),
                  pl.BlockSpec((tb, dh), lambda b, h, p, qt, kt: (b * nb + kt[p], h)),
                  pl.BlockSpec((tb, dh), lambda b, h, p, qt, kt: (b * nb + kt[p], h)),
                  pl.BlockSpec((None, None, 1, tb), lambda b, h, p, qt, kt: (b, h, 0, kt[p])),
                  pl.BlockSpec((None, None, 1, tb), lambda b, h, p, qt, kt: (b, h, 0, qt[p]))],
        out_specs=pl.BlockSpec((tb, dh), lambda b, h, p, qt, kt: (b * nb + qt[p], h)),
        scratch_shapes=[pltpu.VMEM((tb, 1), F32), pltpu.VMEM((tb, 1), F32),
                        pltpu.VMEM((tb, dh), F32), pltpu.VMEM((tb, 1), F32)])
    return pl.pallas_call(
        functools.partial(_fox_kernel, scale=dh ** -0.5),
        grid_spec=grid_spec,
        out_shape=jax.ShapeDtypeStruct((m, heads * dh), BF16),
        compiler_params=_cparams(("parallel", "parallel", "arbitrary")),
        name="fox_attention",
    )(qi_tab, kj_tab, q, k, v, cum, cum)


def _sb_kernel(qi_tab, kj_tab, q_ref, k_ref, v_ref, o_ref, carry_s, acc_s, *, scale):
    p = pl.program_id(2)
    qi, kj = qi_tab[p], kj_tab[p]
    t = q_ref.shape[0]
    ch = min(CUM_CHUNK, t)

    @pl.when(kj == qi)
    def _():
        carry_s[...] = jnp.zeros(carry_s.shape, F32)
        acc_s[...] = jnp.zeros(acc_s.shape, F32)

    def step(diagonal):
        z = _dot_nt(q_ref[...], k_ref[...]) * scale
        log_keep = _log_sigmoid(-z)
        if diagonal:
            r, c = _tri_iota(t)
            strict = c < r
            log_keep = jnp.where(strict, log_keep, 0.0)
        log_take = log_keep + z
        u = _suffix_matrix(ch)
        carry = carry_s[...]
        acc = acc_s[...]
        for c0 in reversed(range(0, t, ch)):
            lk = log_keep[:, c0:c0 + ch]
            tail = _split_dot(lk, u, 2) + carry
            a = jnp.exp(log_take[:, c0:c0 + ch] + tail)
            if diagonal:
                a = jnp.where(strict[:, c0:c0 + ch], a, 0.0)
            acc = acc + _dot(a.astype(BF16), v_ref[c0:c0 + ch, :])
            carry = carry + jnp.sum(lk, axis=1, keepdims=True)
        carry_s[...] = carry
        acc_s[...] = acc

    @pl.when(kj == qi)
    def _():
        step(True)

    @pl.when(kj < qi)
    def _():
        step(False)

    @pl.when(kj == 0)
    def _():
        o_ref[...] = acc_s[...].astype(o_ref.dtype)


def _sb_attention(q, k, v, batch, heads, dh, tb=512):
    m = q.shape[0]
    t = m // batch
    tb = _tile(t, tb)
    nb = t // tb
    qi_tab, kj_tab = _pair_tables(nb, descending=True)
    grid_spec = pltpu.PrefetchScalarGridSpec(
        num_scalar_prefetch=2,
        grid=(batch, heads, int(qi_tab.shape[0])),
        in_specs=[pl.BlockSpec((tb, dh), lambda b, h, p, qt, kt: (b * nb + qt[p], h)),
                  pl.BlockSpec((tb, dh), lambda b, h, p, qt, kt: (b * nb + kt[p], h)),
                  pl.BlockSpec((tb, dh), lambda b, h, p, qt, kt: (b * nb + kt[p], h))],
        out_specs=pl.BlockSpec((tb, dh), lambda b, h, p, qt, kt: (b * nb + qt[p], h)),
        scratch_shapes=[pltpu.VMEM((tb, 1), F32), pltpu.VMEM((tb, dh), F32)])
    return pl.pallas_call(
        functools.partial(_sb_kernel, scale=dh ** -0.5),
        grid_spec=grid_spec,
        out_shape=jax.ShapeDtypeStruct((m, heads * dh), BF16),
        compiler_params=_cparams(("parallel", "parallel", "arbitrary")),
        name="sb_attention",
    )(qi_tab, kj_tab, q, k, v)


def _lambda_value(lq1_ref, lk1_ref, lq2_ref, lk2_ref, lambda_init):
    a = jnp.sum(lq1_ref[...] * lk1_ref[...], axis=1, keepdims=True)
    b = jnp.sum(lq2_ref[...] * lk2_ref[...], axis=1, keepdims=True)
    return jnp.exp(a) - jnp.exp(b) + lambda_init


def _diff_kernel(qi_tab, kj_tab, slope_ref, q_ref, k_ref, v_ref, lq1_ref, lk1_ref, lq2_ref, lk2_ref,
                 g_ref, o_ref, m_s, l_s, acc_s, *, scale, lambda_init, dq):
    h = pl.program_id(1)
    p = pl.program_id(2)
    qi, kj = qi_tab[p], kj_tab[p]
    t = q_ref.shape[0]

    @pl.when(kj == 0)
    def _():
        m_s[...] = jnp.full(m_s.shape, -jnp.inf, F32)
        l_s[...] = jnp.zeros(l_s.shape, F32)
        acc_s[...] = jnp.zeros(acc_s.shape, F32)

    def step(diagonal):
        r, c = _tri_iota(t)
        dist = ((qi - kj) * t + (r - c)).astype(F32)
        bias = -slope_ref[h] * dist
        v = v_ref[...]
        for i in range(2):
            s = _dot_nt(q_ref[:, i * dq:(i + 1) * dq], k_ref[:, i * dq:(i + 1) * dq]) * scale + bias
            if diagonal:
                s = jnp.where(c <= r, s, NEG)
            m_old = m_s[i]
            m_new = jnp.maximum(m_old, jnp.max(s, axis=1, keepdims=True))
            alpha = jnp.exp(m_old - m_new)
            pr = jnp.exp(s - m_new)
            l_s[i] = alpha * l_s[i] + jnp.sum(pr, axis=1, keepdims=True)
            acc_s[i] = alpha * acc_s[i] + _dot(pr.astype(BF16), v)
            m_s[i] = m_new

    @pl.when(kj < qi)
    def _():
        step(False)

    @pl.when(kj == qi)
    def _():
        step(True)
        lam = _lambda_value(lq1_ref, lk1_ref, lq2_ref, lk2_ref, lambda_init)
        o = acc_s[0] / l_s[0] - lam * (acc_s[1] / l_s[1])
        o = o * lax.rsqrt(jnp.mean(o * o, axis=1, keepdims=True) + LN_EPS) * g_ref[...]
        o_ref[...] = (o * (1.0 - lambda_init)).astype(o_ref.dtype)


def _diff_attention(q, k, v, slopes, lq1, lk1, lq2, lk2, g, batch, heads, dq, lambda_init, tb=512):
    m = q.shape[0]
    t = m // batch
    tb = _tile(t, tb)
    nb = t // tb
    dv = 2 * dq
    qi_tab, kj_tab = _pair_tables(nb, descending=False)
    vec = pl.BlockSpec((1, dq), lambda b, h, p, qt, kt, sl: (0, 0))
    grid_spec = pltpu.PrefetchScalarGridSpec(
        num_scalar_prefetch=3,
        grid=(batch, heads, int(qi_tab.shape[0])),
        in_specs=[pl.BlockSpec((tb, dv), lambda b, h, p, qt, kt, sl: (b * nb + qt[p], h)),
                  pl.BlockSpec((tb, dv), lambda b, h, p, qt, kt, sl: (b * nb + kt[p], h)),
                  pl.BlockSpec((tb, dv), lambda b, h, p, qt, kt, sl: (b * nb + kt[p], h)),
                  vec, vec, vec, vec,
                  pl.BlockSpec((1, dv), lambda b, h, p, qt, kt, sl: (0, 0))],
        out_specs=pl.BlockSpec((tb, dv), lambda b, h, p, qt, kt, sl: (b * nb + qt[p], h)),
        scratch_shapes=[pltpu.VMEM((2, tb, 1), F32), pltpu.VMEM((2, tb, 1), F32),
                        pltpu.VMEM((2, tb, dv), F32)])
    return pl.pallas_call(
        functools.partial(_diff_kernel, scale=dq ** -0.5, lambda_init=lambda_init, dq=dq),
        grid_spec=grid_spec,
        out_shape=jax.ShapeDtypeStruct((m, heads * dv), BF16),
        compiler_params=_cparams(("parallel", "parallel", "arbitrary")),
        name="diff_attention",
    )(qi_tab, kj_tab, slopes, q, k, v, lq1, lk1, lq2, lk2, g)


def _page_scores(q, k_ref, heads, dq, n_maps):
    rows = q.shape[0]
    page = k_ref.shape[0]
    rid = lax.broadcasted_iota(jnp.int32, (rows, page), 0)
    s = jnp.zeros((rows, page), F32)
    for h in range(heads):
        for i in range(n_maps):
            kh = k_ref[:, h, i * dq:(i + 1) * dq].astype(BF16)
            s = jnp.where(rid == i * heads + h, _dot_nt(q, kh), s)
    return s


def _accumulate_pv(acc_s, alpha, pr, v_refs, heads):
    page = v_refs[0].shape[0]
    for h in range(heads):
        pv = None
        for g, v_ref in enumerate(v_refs):
            d = _dot(pr[:, g * page:(g + 1) * page], v_ref[:, h, :].astype(BF16))
            pv = d if pv is None else pv + d
        acc_s[h] = pv + (acc_s[h] if alpha is None else alpha * acc_s[h])


def _own_rows(acc_s, heads, row0):
    d = acc_s.shape[2]
    rid = lax.broadcasted_iota(jnp.int32, (heads, d), 0)
    o = jnp.zeros((heads, d), F32)
    for h in range(heads):
        o = jnp.where(rid == h, acc_s[h, row0:row0 + heads, :], o)
    return o


def _fox_dec_kernel(pt_ref, q_ref, kn_ref, vn_ref, lfn_ref, *refs, g_pages, scale, heads, dh):
    lf_refs = refs[:g_pages]
    k_refs = refs[g_pages:2 * g_pages]
    v_refs = refs[2 * g_pages:3 * g_pages]
    o_ref, m_s, l_s, acc_s, carry_s = refs[3 * g_pages:]
    j = pl.program_id(1)
    page = k_refs[0].shape[0]
    q = q_ref[...].astype(BF16)

    @pl.when(j == 0)
    def _():
        qn = q.astype(F32) * kn_ref[...].astype(BF16).astype(F32)
        m_s[...] = jnp.sum(qn, axis=1, keepdims=True) * scale
        l_s[...] = jnp.ones(l_s.shape, F32)
        vn = vn_ref[...].astype(BF16).astype(F32)
        for h in range(heads):
            acc_s[h] = vn
        carry_s[...] = lfn_ref[...]

    u = _suffix_matrix(page)
    carry = carry_s[...]
    logits = []
    for g in range(g_pages):
        lf = lf_refs[g][...]
        cum = _split_dot(lf, u, 3) + carry
        logits.append(_page_scores(q, k_refs[g], heads, dh, 1) * scale + cum)
        carry = carry + jnp.sum(lf, axis=1, keepdims=True)
    carry_s[...] = carry
    s = jnp.concatenate(logits, axis=1) if g_pages > 1 else logits[0]
    m_old = m_s[...]
    m_new = jnp.maximum(m_old, jnp.max(s, axis=1, keepdims=True))
    alpha = jnp.exp(m_old - m_new)
    pr = jnp.exp(s - m_new)
    l_s[...] = alpha * l_s[...] + jnp.sum(pr, axis=1, keepdims=True)
    _accumulate_pv(acc_s, alpha, pr.astype(BF16), v_refs, heads)
    m_s[...] = m_new

    @pl.when(j == pl.num_programs(1) - 1)
    def _():
        o_ref[...] = (_own_rows(acc_s, heads, 0) / l_s[...]).astype(o_ref.dtype)


def _sb_dec_kernel(pt_ref, q_ref, *refs, g_pages, scale, heads, dh):
    k_refs = refs[:g_pages]
    v_refs = refs[g_pages:2 * g_pages]
    o_ref, acc_s, carry_s = refs[2 * g_pages:]
    j = pl.program_id(1)
    page = k_refs[0].shape[0]
    q = q_ref[...].astype(BF16)

    @pl.when(j == 0)
    def _():
        acc_s[...] = jnp.zeros(acc_s.shape, F32)
        carry_s[...] = jnp.zeros(carry_s.shape, F32)

    u = _suffix_matrix(page)
    carry = carry_s[...]
    weights = []
    for g in range(g_pages):
        z = _page_scores(q, k_refs[g], heads, dh, 1) * scale
        log_keep = _log_sigmoid(-z)
        tail = _split_dot(log_keep, u, 2) + carry
        weights.append(jnp.exp(log_keep + z + tail))
        carry = carry + jnp.sum(log_keep, axis=1, keepdims=True)
    carry_s[...] = carry
    a = jnp.concatenate(weights, axis=1) if g_pages > 1 else weights[0]
    _accumulate_pv(acc_s, None, a.astype(BF16), v_refs, heads)

    @pl.when(j == pl.num_programs(1) - 1)
    def _():
        o_ref[...] = _own_rows(acc_s, heads, 0).astype(o_ref.dtype)


def _diff_dec_kernel(pt_ref, q_ref, kn_ref, vn_ref, slope_ref, lq1_ref, lk1_ref, lq2_ref, lk2_ref,
                     g_ref, *refs, g_pages, scale, heads, dq, lambda_init, n_pages):
    k_refs = refs[:g_pages]
    v_refs = refs[g_pages:2 * g_pages]
    o_ref, m_s, l_s, acc_s = refs[2 * g_pages:]
    j = pl.program_id(1)
    page = k_refs[0].shape[0]
    rows = 2 * heads
    q = q_ref[...].astype(BF16)

    @pl.when(j == 0)
    def _():
        qn = q.astype(F32) * kn_ref[...].astype(BF16).astype(F32)
        m_s[...] = jnp.sum(qn, axis=1, keepdims=True) * scale
        l_s[...] = jnp.ones(l_s.shape, F32)
        vn = vn_ref[...].astype(BF16).astype(F32)
        vn2 = jnp.concatenate([vn, vn], axis=0)
        for h in range(heads):
            acc_s[h] = vn2

    logits = []
    for g in range(g_pages):
        first = (n_pages - 1 - (j * g_pages + g)) * page
        kpos = first + lax.broadcasted_iota(jnp.int32, (rows, page), 1)
        dist = (n_pages * page - kpos).astype(F32)
        logits.append(_page_scores(q, k_refs[g], heads, dq, 2) * scale - slope_ref[...] * dist)
    s = jnp.concatenate(logits, axis=1) if g_pages > 1 else logits[0]
    m_old = m_s[...]
    m_new = jnp.maximum(m_old, jnp.max(s, axis=1, keepdims=True))
    alpha = jnp.exp(m_old - m_new)
    pr = jnp.exp(s - m_new)
    l_s[...] = alpha * l_s[...] + jnp.sum(pr, axis=1, keepdims=True)
    _accumulate_pv(acc_s, alpha, pr.astype(BF16), v_refs, heads)
    m_s[...] = m_new

    @pl.when(j == pl.num_programs(1) - 1)
    def _():
        lam = _lambda_value(lq1_ref, lk1_ref, lq2_ref, lk2_ref, lambda_init)
        l = l_s[...]
        o = _own_rows(acc_s, heads, 0) / l[:heads] - lam * (_own_rows(acc_s, heads, heads) / l[heads:])
        o = o * lax.rsqrt(jnp.mean(o * o, axis=1, keepdims=True) + LN_EPS) * g_ref[...]
        o_ref[...] = (o * (1.0 - lambda_init)).astype(o_ref.dtype)


def _page_specs(n_pages, g_pages, block, layer):
    def make(g):
        def index(b, j, pt):
            return (layer, pt[b, n_pages - 1 - (j * g_pages + g)]) + (0,) * len(block)
        return pl.BlockSpec((None, None) + block, index)
    return [make(g) for g in range(g_pages)]


def _per_seq_spec(shape):
    return pl.BlockSpec((None,) + shape, lambda b, j, pt: (b,) + (0,) * len(shape))


def _shared_spec(shape):
    return pl.BlockSpec(shape, lambda b, j, pt: (0,) * len(shape))


def _fox_decode(page_table, q, kn, vn, lfn, lf_cache, k_cache, v_cache, layer, g_pages=4):
    bsz, n_pages = page_table.shape
    _, _, page, heads, dh = k_cache.shape
    g_pages = _tile(n_pages, g_pages)
    grid_spec = pltpu.PrefetchScalarGridSpec(
        num_scalar_prefetch=1,
        grid=(bsz, n_pages // g_pages),
        in_specs=([_per_seq_spec((heads, dh))] * 3 + [_per_seq_spec((heads, 1))]
                  + _page_specs(n_pages, g_pages, (heads, page), layer)
                  + _page_specs(n_pages, g_pages, (page, heads, dh), layer)
                  + _page_specs(n_pages, g_pages, (page, heads, dh), layer)),
        out_specs=_per_seq_spec((heads, dh)),
        scratch_shapes=[pltpu.VMEM((heads, 1), F32), pltpu.VMEM((heads, 1), F32),
                        pltpu.VMEM((heads, heads, dh), F32), pltpu.VMEM((heads, 1), F32)])
    return pl.pallas_call(
        functools.partial(_fox_dec_kernel, g_pages=g_pages, scale=dh ** -0.5, heads=heads, dh=dh),
        grid_spec=grid_spec,
        out_shape=jax.ShapeDtypeStruct((bsz, heads, dh), BF16),
        compiler_params=_cparams(("parallel", "arbitrary")),
        name="fox_decode",
    )(page_table, q, kn, vn, lfn, *([lf_cache] * g_pages), *([k_cache] * g_pages), *([v_cache] * g_pages))


def _sb_decode(page_table, q, k_cache, v_cache, layer, g_pages=4):
    bsz, n_pages = page_table.shape
    _, _, page, heads, dh = k_cache.shape
    g_pages = _tile(n_pages, g_pages)
    grid_spec = pltpu.PrefetchScalarGridSpec(
        num_scalar_prefetch=1,
        grid=(bsz, n_pages // g_pages),
        in_specs=([_per_seq_spec((heads, dh))]
                  + _page_specs(n_pages, g_pages, (page, heads, dh), layer)
                  + _page_specs(n_pages, g_pages, (page, heads, dh), layer)),
        out_specs=_per_seq_spec((heads, dh)),
        scratch_shapes=[pltpu.VMEM((heads, heads, dh), F32), pltpu.VMEM((heads, 1), F32)])
    return pl.pallas_call(
        functools.partial(_sb_dec_kernel, g_pages=g_pages, scale=dh ** -0.5, heads=heads, dh=dh),
        grid_spec=grid_spec,
        out_shape=jax.ShapeDtypeStruct((bsz, heads, dh), BF16),
        compiler_params=_cparams(("parallel", "arbitrary")),
        name="sb_decode",
    )(page_table, q, *([k_cache] * g_pages), *([v_cache] * g_pages))


def _diff_decode(page_table, q, kn, vn, slopes, lq1, lk1, lq2, lk2, g, k_cache, v_cache, layer, lambda_init,
                 g_pages=2):
    bsz, n_pages = page_table.shape
    _, _, page, heads, dv = k_cache.shape
    dq = dv // 2
    rows = 2 * heads
    g_pages = _tile(n_pages, g_pages)
    grid_spec = pltpu.PrefetchScalarGridSpec(
        num_scalar_prefetch=1,
        grid=(bsz, n_pages // g_pages),
        in_specs=([_per_seq_spec((rows, dq)), _per_seq_spec((rows, dq)), _per_seq_spec((heads, dv)),
                   _shared_spec((rows, 1)),
                   _shared_spec((1, dq)), _shared_spec((1, dq)), _shared_spec((1, dq)), _shared_spec((1, dq)),
                   _shared_spec((1, dv))]
                  + _page_specs(n_pages, g_pages, (page, heads, dv), layer)
                  + _page_specs(n_pages, g_pages, (page, heads, dv), layer)),
        out_specs=_per_seq_spec((heads, dv)),
        scratch_shapes=[pltpu.VMEM((rows, 1), F32), pltpu.VMEM((rows, 1), F32),
                        pltpu.VMEM((heads, rows, dv), F32)])
    return pl.pallas_call(
        functools.partial(_diff_dec_kernel, g_pages=g_pages, scale=dq ** -0.5, heads=heads, dq=dq,
                          lambda_init=lambda_init, n_pages=n_pages),
        grid_spec=grid_spec,
        out_shape=jax.ShapeDtypeStruct((bsz, heads, dv), BF16),
        compiler_params=_cparams(("parallel", "arbitrary")),
        name="diff_decode",
    )(page_table, q, kn, vn, slopes, lq1, lk1, lq2, lk2, g, *([k_cache] * g_pages), *([v_cache] * g_pages))


def _even_layer(xpb, xsb, caches, page_table, w_in, b_f, w_o, dims):
    cache_k_a, cache_v_a, cache_lf, cache_k_b, cache_v_b, layer_idx = caches
    bsz, t, dbs, heads, dh = dims
    w = heads * dh
    wb = w_in.astype(BF16)
    seg = [wb[:, i * w:(i + 1) * w] for i in range(3)] + [wb[:, 3 * w + heads + i * w:3 * w + heads + (i + 1) * w]
                                                         for i in range(3)]
    w_f = jnp.pad(wb[:, 3 * w:3 * w + heads], ((0, 0), (0, LANES - heads)))
    b_pad = jnp.pad(b_f.astype(F32), (0, LANES - heads)).reshape(1, LANES)
    wo_a, wo_b = w_o[:w].astype(BF16), w_o[w:].astype(BF16)

    def project(xb):
        (q_a,) = _mm(xb, seg[0], (BF16,))
        k_a, k_ab = _mm(xb, seg[1], (F32, BF16))
        v_a, v_ab = _mm(xb, seg[2], (F32, BF16))
        (q_b,) = _mm(xb, seg[3], (BF16,))
        k_b, k_bb = _mm(xb, seg[4], (F32, BF16))
        v_b, v_bb = _mm(xb, seg[5], (F32, BF16))
        lf = _forget_gate(xb, w_f, b_pad)[:, :heads]
        return (q_a, k_a, k_ab, v_a, v_ab, q_b, k_b, k_bb, v_b, v_bb, lf)

    q_a, k_a, k_ab, v_a, v_ab, q_b, k_b, k_bb, v_b, v_bb, lf = project(xpb)
    lf_t = jnp.swapaxes(lf.reshape(bsz, t, heads), 1, 2)
    cum = _suffix_sums(lf_t.reshape(bsz * heads, t)).reshape(bsz, heads, 1, t)
    o_a = _fox_attention(q_a, k_ab, v_ab, cum, bsz, heads, dh)
    o_b = _sb_attention(q_b, k_bb, v_bb, bsz, heads, dh)
    new_p = (k_a.reshape(bsz, t, heads, dh), v_a.reshape(bsz, t, heads, dh), lf.reshape(bsz, t, heads),
             k_b.reshape(bsz, t, heads, dh), v_b.reshape(bsz, t, heads, dh))

    sq_a, sk_a, _, sv_a, _, sq_b, sk_b, _, sv_b, _, slf = project(xsb)
    per_head = lambda a: a.astype(F32).reshape(dbs, heads, dh)
    so_a = _fox_decode(page_table, per_head(sq_a), per_head(sk_a), per_head(sv_a), slf.reshape(dbs, heads, 1),
                       jnp.swapaxes(cache_lf, 2, 3), cache_k_a, cache_v_a, layer_idx)
    so_b = _sb_decode(page_table, per_head(sq_b), cache_k_b, cache_v_b, layer_idx)
    new_s = (sk_a.reshape(dbs, 1, heads, dh), sv_a.reshape(dbs, 1, heads, dh), slf.reshape(dbs, 1, heads),
             sk_b.reshape(dbs, 1, heads, dh), sv_b.reshape(dbs, 1, heads, dh))
    return (o_a, o_b), (so_a.reshape(dbs, w), so_b.reshape(dbs, w)), (wo_a, wo_b), new_p, new_s


def _odd_layer(xpb, xsb, caches, page_table, w_in, lq1, lk1, lq2, lk2, gain, w_o, lambda_init, dims):
    cache_k, cache_v, layer_idx = caches
    bsz, t, dbs, heads, dq = dims
    dv = 2 * dq
    w = heads * dv
    wb = w_in.astype(BF16)
    seg = [wb[:, i * w:(i + 1) * w] for i in range(3)]
    slopes = 2.0 ** (-8.0 * jnp.arange(1, heads + 1, dtype=F32) / heads)
    vecs = [a.astype(F32).reshape(1, dq) for a in (lq1, lk1, lq2, lk2)]
    g_row = gain.astype(F32).reshape(1, dv)

    def project(xb):
        (q,) = _mm(xb, seg[0], (BF16,))
        k, kb = _mm(xb, seg[1], (F32, BF16))
        v, vb = _mm(xb, seg[2], (F32, BF16))
        return q, k, kb, v, vb

    q, k, kb, v, vb = project(xpb)
    o = _diff_attention(q, kb, vb, slopes, *vecs, g_row, bsz, heads, dq, lambda_init)
    new_p = (k.reshape(bsz, t, heads, dv), v.reshape(bsz, t, heads, dv))

    sq, sk, _, sv, _ = project(xsb)
    sq4 = sq.astype(F32).reshape(dbs, heads, 2, dq)
    sk4 = sk.reshape(dbs, heads, 2, dq)
    q_rows = jnp.concatenate([sq4[:, :, 0], sq4[:, :, 1]], axis=1)
    k_rows = jnp.concatenate([sk4[:, :, 0], sk4[:, :, 1]], axis=1)
    slope_rows = jnp.concatenate([slopes, slopes]).reshape(2 * heads, 1)
    so = _diff_decode(page_table, q_rows, k_rows, sv.reshape(dbs, heads, dv), slope_rows, *vecs, g_row,
                      cache_k, cache_v, layer_idx, lambda_init)
    new_s = (sk.reshape(dbs, 1, heads, dv), sv.reshape(dbs, 1, heads, dv))
    return (o,), (so.reshape(dbs, w),), (w_o.astype(BF16),), new_p, new_s


def kernel(x_prompt, x_sample, cache_fox_k, cache_fox_v, cache_fox_logf, cache_sb_k, cache_sb_v, cache_diff_k, cache_diff_v, page_table, w_in_even, b_forget, w_out_even, w_in_odd, lambda_q1, lambda_k1, lambda_q2, lambda_k2, subln_gain, w_out_odd, ln_mix_gain, ln_mix_bias, ln_ffn_gain, ln_ffn_bias, w_ffn_up, w_ffn_down):
    bsz, t, d = x_prompt.shape
    dbs, ts, _ = x_sample.shape
    assert ts == 1, "the sample group carries one new token per sequence"
    depth = ln_mix_gain.shape[0]
    alpha = (2 * depth) ** 0.25
    h_fox, dh = cache_fox_k.shape[3], cache_fox_k.shape[4]
    h_diff, dq = cache_diff_k.shape[3], cache_diff_k.shape[4] // 2
    assert cache_sb_k.shape[3:] == (h_fox, dh), "both head groups of an even layer share one geometry"

    xp = x_prompt.reshape(bsz * t, d)
    xs = x_sample.reshape(dbs, d)
    xpb, xsb = xp.astype(BF16), xs.astype(BF16)
    even_p, even_s, odd_p, odd_s = [], [], [], []
    row = lambda a: a.astype(F32).reshape(1, d)
    for layer in range(depth):
        i = layer // 2
        if layer % 2 == 0:
            caches = (cache_fox_k, cache_fox_v, cache_fox_logf, cache_sb_k, cache_sb_v, i)
            o_p, o_s, w_o, new_p, new_s = _even_layer(xpb, xsb, caches, page_table, w_in_even[i],
                                                      b_forget[i], w_out_even[i], (bsz, t, dbs, h_fox, dh))
            even_p.append(new_p)
            even_s.append(new_s)
        else:
            lambda_init = 0.8 - 0.6 * math.exp(-0.3 * layer)
            caches = (cache_diff_k, cache_diff_v, i)
            o_p, o_s, w_o, new_p, new_s = _odd_layer(xpb, xsb, caches, page_table, w_in_odd[i], lambda_q1[i],
                                                     lambda_k1[i], lambda_q2[i], lambda_k2[i], subln_gain[i],
                                                     w_out_odd[i], lambda_init, (bsz, t, dbs, h_diff, dq))
            odd_p.append(new_p)
            odd_s.append(new_s)
        g1, b1 = row(ln_mix_gain[layer]), row(ln_mix_bias[layer])
        g2, b2 = row(ln_ffn_gain[layer]), row(ln_ffn_bias[layer])
        w_up, w_down = w_ffn_up[layer].astype(BF16), w_ffn_down[layer].astype(BF16)
        xp, xpb = _proj_residual_ln(list(o_p), list(w_o), xp, g1, b1, alpha)
        xs, xsb = _proj_residual_ln(list(o_s), list(w_o), xs, g1, b1, alpha)
        xp, xpb = _ffn_residual_ln(xpb, xp, w_up, w_down, g2, b2, alpha)
        xs, xsb = _ffn_residual_ln(xsb, xs, w_up, w_down, g2, b2, alpha)

    def stk(rows, j):
        return jnp.stack([r[j] for r in rows])

    return (xp.reshape(bsz, t, d), xs.reshape(dbs, 1, d),
            stk(even_p, 0), stk(even_p, 1), stk(even_p, 2), stk(even_p, 3), stk(even_p, 4), stk(odd_p, 0), stk(odd_p, 1),
            stk(even_s, 0), stk(even_s, 1), stk(even_s, 2), stk(even_s, 3), stk(even_s, 4), stk(odd_s, 0), stk(odd_s, 1))
```

```python
import functools
import math

import jax
import jax.numpy as jnp
from jax import lax
from jax.experimental import pallas as pl
from jax.experimental.pallas import tpu as pltpu

F32 = jnp.float32
BF16 = jnp.bfloat16
LN_EPS = 1e-5
NEG = -1e30
LANES = 128
CUM_CHUNK = 256
VMEM_LIMIT = 56 * 1024 * 1024


def _cparams(sem):
    return pltpu.CompilerParams(dimension_semantics=sem, vmem_limit_bytes=VMEM_LIMIT)


def _tile(n, pref):
    t = min(n, pref)
    while n % t:
        t //= 2
    return t


def _dot(a, b):
    return jnp.dot(a, b, preferred_element_type=F32)


def _dot_nt(a, b):
    return lax.dot_general(a, b, (((1,), (1,)), ((), ())), preferred_element_type=F32)


def _bf16_terms(x, terms):
    out = []
    r = x
    for t in range(terms):
        h = r.astype(BF16).astype(F32)
        out.append(h)
        if t + 1 < terms:
            r = r - h
    return out


def _split_dot(x, u, terms):
    out = None
    for h in _bf16_terms(x, terms):
        d = _dot(h.astype(BF16), u)
        out = d if out is None else out + d
    return out


def _suffix_matrix(n):
    r = lax.broadcasted_iota(jnp.int32, (n, n), 0)
    c = lax.broadcasted_iota(jnp.int32, (n, n), 1)
    return jnp.where(r > c, 1.0, 0.0).astype(BF16)


def _log_sigmoid(x):
    return jnp.minimum(x, 0.0) - jnp.log1p(jnp.exp(-jnp.abs(x)))


def _layer_norm(x, g, b):
    mu = jnp.mean(x, axis=-1, keepdims=True)
    xc = x - mu
    var = jnp.mean(xc * xc, axis=-1, keepdims=True)
    return xc * lax.rsqrt(var + LN_EPS) * g + b


def _mm_kernel(x_ref, w_ref, *o_refs, out_scale):
    acc = _dot(x_ref[...], w_ref[...])
    if out_scale != 1.0:
        acc = acc * out_scale
    for o_ref in o_refs:
        o_ref[...] = acc.astype(o_ref.dtype)


def _mm(x, w, out_dtypes, out_scale=1.0, tm=512, tn=1024):
    m, k = x.shape
    n = w.shape[1]
    tm, tn = _tile(m, tm), _tile(n, tn)
    outs = pl.pallas_call(
        functools.partial(_mm_kernel, out_scale=out_scale),
        grid=(n // tn, m // tm),
        in_specs=[pl.BlockSpec((tm, k), lambda j, i: (i, 0)),
                  pl.BlockSpec((k, tn), lambda j, i: (0, j))],
        out_specs=[pl.BlockSpec((tm, tn), lambda j, i: (i, j)) for _ in out_dtypes],
        out_shape=[jax.ShapeDtypeStruct((m, n), d) for d in out_dtypes],
        compiler_params=_cparams(("parallel", "parallel")),
        name="proj",
    )(x, w)
    return outs


def _gate_kernel(x_ref, w_ref, b_ref, o_ref):
    o_ref[...] = _log_sigmoid(_dot(x_ref[...], w_ref[...]) + b_ref[...])


def _forget_gate(x, w, b, tm=512):
    m, k = x.shape
    n = w.shape[1]
    tm = _tile(m, tm)
    return pl.pallas_call(
        _gate_kernel,
        grid=(m // tm,),
        in_specs=[pl.BlockSpec((tm, k), lambda i: (i, 0)),
                  pl.BlockSpec((k, n), lambda i: (0, 0)),
                  pl.BlockSpec((1, n), lambda i: (0, 0))],
        out_specs=pl.BlockSpec((tm, n), lambda i: (i, 0)),
        out_shape=jax.ShapeDtypeStruct((m, n), F32),
        compiler_params=_cparams(("parallel",)),
        name="forget_gate",
    )(x, w, b)


def _revcum_kernel(x_ref, o_ref):
    rows, t = x_ref.shape
    ch = min(CUM_CHUNK, t)
    u = _suffix_matrix(ch)
    carry = jnp.zeros((rows, 1), F32)
    for c in reversed(range(t // ch)):
        xc = x_ref[:, c * ch:(c + 1) * ch]
        o_ref[:, c * ch:(c + 1) * ch] = _split_dot(xc, u, 3) + carry
        carry = carry + jnp.sum(xc, axis=1, keepdims=True)


def _suffix_sums(x):
    return pl.pallas_call(
        _revcum_kernel,
        out_shape=jax.ShapeDtypeStruct(x.shape, F32),
        name="suffix_sums",
    )(x)


def _proj_ln_kernel(*refs, n_pairs, alpha):
    a_refs = refs[:n_pairs]
    w_refs = refs[n_pairs:2 * n_pairs]
    x_ref, g_ref, b_ref, y_ref, yb_ref = refs[2 * n_pairs:]
    acc = alpha * x_ref[...]
    for a_ref, w_ref in zip(a_refs, w_refs):
        acc = acc + _dot(a_ref[...], w_ref[...])
    y = _layer_norm(acc, g_ref[...], b_ref[...])
    y_ref[...] = y
    yb_ref[...] = y.astype(BF16)


def _proj_residual_ln(a_list, w_list, x, g, b, alpha, tm=256):
    m, d = x.shape
    tm = _tile(m, tm)
    n_pairs = len(a_list)
    in_specs = ([pl.BlockSpec((tm, a.shape[1]), lambda i: (i, 0)) for a in a_list]
                + [pl.BlockSpec(w.shape, lambda i: (0, 0)) for w in w_list]
                + [pl.BlockSpec((tm, d), lambda i: (i, 0)),
                   pl.BlockSpec((1, d), lambda i: (0, 0)),
                   pl.BlockSpec((1, d), lambda i: (0, 0))])
    return pl.pallas_call(
        functools.partial(_proj_ln_kernel, n_pairs=n_pairs, alpha=alpha),
        grid=(m // tm,),
        in_specs=in_specs,
        out_specs=[pl.BlockSpec((tm, d), lambda i: (i, 0)), pl.BlockSpec((tm, d), lambda i: (i, 0))],
        out_shape=[jax.ShapeDtypeStruct((m, d), F32), jax.ShapeDtypeStruct((m, d), BF16)],
        compiler_params=_cparams(("parallel",)),
        name="out_proj_ln",
    )(*a_list, *w_list, x, g, b)


def _ffn_kernel(xb_ref, x_ref, wu_ref, wd_ref, g_ref, b_ref, y_ref, yb_ref, acc_ref, *, alpha):
    f = pl.program_id(1)

    @pl.when(f == 0)
    def _():
        acc_ref[...] = alpha * x_ref[...]

    h = jnp.maximum(_dot(xb_ref[...], wu_ref[...]), 0.0)
    acc_ref[...] += _dot((h * h).astype(BF16), wd_ref[...])

    @pl.when(f == pl.num_programs(1) - 1)
    def _():
        y = _layer_norm(acc_ref[...], g_ref[...], b_ref[...])
        y_ref[...] = y
        yb_ref[...] = y.astype(BF16)


def _ffn_residual_ln(xb, x, w_up, w_down, g, b, alpha, tm=512, tf=512):
    m, d = x.shape
    dff = w_up.shape[1]
    tm, tf = _tile(m, tm), _tile(dff, tf)
    return pl.pallas_call(
        functools.partial(_ffn_kernel, alpha=alpha),
        grid=(m // tm, dff // tf),
        in_specs=[pl.BlockSpec((tm, d), lambda i, f: (i, 0)),
                  pl.BlockSpec((tm, d), lambda i, f: (i, 0)),
                  pl.BlockSpec((d, tf), lambda i, f: (0, f)),
                  pl.BlockSpec((tf, d), lambda i, f: (f, 0)),
                  pl.BlockSpec((1, d), lambda i, f: (0, 0)),
                  pl.BlockSpec((1, d), lambda i, f: (0, 0))],
        out_specs=[pl.BlockSpec((tm, d), lambda i, f: (i, 0)), pl.BlockSpec((tm, d), lambda i, f: (i, 0))],
        out_shape=[jax.ShapeDtypeStruct((m, d), F32), jax.ShapeDtypeStruct((m, d), BF16)],
        scratch_shapes=[pltpu.VMEM((tm, d), F32)],
        compiler_params=_cparams(("parallel", "arbitrary")),
        name="ffn_ln",
    )(xb, x, w_up, w_down, g, b)


def _pair_tables(n_blocks, descending):
    qi, kj = [], []
    for q in range(n_blocks):
        ks = range(q, -1, -1) if descending else range(q + 1)
        for k in ks:
            qi.append(q)
            kj.append(k)
    return jnp.asarray(qi, jnp.int32), jnp.asarray(kj, jnp.int32)


def _tri_iota(t):
    r = lax.broadcasted_iota(jnp.int32, (t, t), 0)
    c = lax.broadcasted_iota(jnp.int32, (t, t), 1)
    return r, c


def _lanes_to_rows(row, t):
    r, c = _tri_iota(t)
    col = jnp.sum(jnp.where(r == c, row, 0.0), axis=1, keepdims=True)
    return jnp.broadcast_to(col, (t, LANES))


def _softmax_step(a, row_shift, m_prev, l_prev, acc_prev, v):
    t, tk = a.shape
    m_next = jnp.maximum(m_prev, jnp.max(a, axis=1, keepdims=True) - row_shift)
    pr = jnp.exp(a - jnp.tile(m_next + row_shift, (1, tk // LANES)))
    alpha = jnp.exp(m_prev - m_next)
    l_next = alpha * l_prev + jnp.sum(pr, axis=1, keepdims=True)
    acc_next = jnp.tile(alpha, (1, v.shape[1] // LANES)) * acc_prev + _dot(pr.astype(BF16), v)
    return m_next, l_next, acc_next


def _fox_kernel(qi_tab, kj_tab, q_ref, k_ref, v_ref, ck_ref, cqrow_ref, o_ref, m_s, l_s, acc_s, cq_s):
    p = pl.program_id(2)
    qi, kj = qi_tab[p], kj_tab[p]
    t = q_ref.shape[0]

    @pl.when(kj == 0)
    def _():
        m_s[...] = jnp.full(m_s.shape, -jnp.inf, F32)
        l_s[...] = jnp.zeros(l_s.shape, F32)
        acc_s[...] = jnp.zeros(acc_s.shape, F32)
        cq_s[...] = _lanes_to_rows(cqrow_ref[...], t)

    def step(diagonal):
        a = _dot_nt(q_ref[...], k_ref[...]) + ck_ref[...]
        if diagonal:
            r, c = _tri_iota(t)
            a = jnp.where(c <= r, a, NEG)
        m, l, acc = _softmax_step(a, cq_s[...], m_s[...], l_s[...], acc_s[...], v_ref[...])
        m_s[...] = m
        l_s[...] = l
        acc_s[...] = acc

    @pl.when(kj < qi)
    def _():
        step(False)

    @pl.when(kj == qi)
    def _():
        step(True)
        o_ref[...] = (acc_s[...] / l_s[...]).astype(o_ref.dtype)


def _fox_attention(q, k, v, cum, batch, heads, dh, tb=512):
    assert dh == LANES
    m = q.shape[0]
    t = m // batch
    tb = _tile(t, tb)
    nb = t // tb
    qi_tab, kj_tab = _pair_tables(nb, descending=False)
    grid_spec = pltpu.PrefetchScalarGridSpec(
        num_scalar_prefetch=2,
        grid=(batch, heads, int(qi_tab.shape[0])),
        in_specs=[pl.BlockSpec((tb, dh), lambda b, h, p, qt, kt: (b * nb + qt[p], h)),
                  pl.BlockSpec((tb, dh), lambda b, h, p, qt, kt: (b * nb + kt[p], h)),
                  pl.BlockSpec((tb, dh), lambda b, h, p, qt, kt: (b * nb + kt[p], h)),
                  pl.BlockSpec((None, None, 1, tb), lambda b, h, p, qt, kt: (b, h, 0, kt[p])),
                  pl.BlockSpec((None, None, 1, tb), lambda b, h, p, qt, kt: (b, h, 0, qt[p]))],
        out_specs=pl.BlockSpec((tb, dh), lambda b, h, p, qt, kt: (b * nb + qt[p], h)),
        scratch_shapes=[pltpu.VMEM((tb, LANES), F32), pltpu.VMEM((tb, LANES), F32),
                        pltpu.VMEM((tb, dh), F32), pltpu.VMEM((tb, LANES), F32)])
    return pl.pallas_call(
        _fox_kernel,
        grid_spec=grid_spec,
        out_shape=jax.ShapeDtypeStruct((m, heads * dh), BF16),
        compiler_params=_cparams(("parallel", "parallel", "arbitrary")),
        name="fox_attention",
    )(qi_tab, kj_tab, q, k, v, cum, cum)


def _sb_kernel(qi_tab, kj_tab, q_ref, k_ref, v_ref, o_ref, carry_s, acc_s):
    p = pl.program_id(2)
    qi, kj = qi_tab[p], kj_tab[p]
    t = q_ref.shape[0]
    ch = min(CUM_CHUNK, t)

    @pl.when(kj == qi)
    def _():
        carry_s[...] = jnp.zeros(carry_s.shape, F32)
        acc_s[...] = jnp.zeros(acc_s.shape, F32)

    def step(diagonal):
        u = _suffix_matrix(ch)
        drop = carry_s[...]
        acc = acc_s[...]
        for c0 in reversed(range(0, t, ch)):
            z = _dot_nt(q_ref[...], k_ref[c0:c0 + ch, :])
            sp = jnp.maximum(z, 0.0) + jnp.log(1.0 + jnp.exp(-jnp.abs(z)))
            if diagonal:
                r = lax.broadcasted_iota(jnp.int32, (t, ch), 0)
                c = lax.broadcasted_iota(jnp.int32, (t, ch), 1) + c0
                strict = c < r
                sp = jnp.where(strict, sp, 0.0)
            hi, lo = (x.astype(BF16) for x in _bf16_terms(sp, 2))
            a = jnp.exp(z - sp - (_dot(hi, u) + _dot(lo, u)) - jnp.tile(drop, (1, ch // LANES)))
            if diagonal:
                a = jnp.where(strict, a, 0.0)
            acc = acc + _dot(a.astype(BF16), v_ref[c0:c0 + ch, :])
            drop = drop + jnp.sum(sp, axis=1, keepdims=True)
        carry_s[...] = drop
        acc_s[...] = acc

    @pl.when(kj == qi)
    def _():
        step(True)

    @pl.when(kj < qi)
    def _():
        step(False)

    @pl.when(kj == 0)
    def _():
        o_ref[...] = acc_s[...].astype(o_ref.dtype)


def _sb_attention(q, k, v, batch, heads, dh, tb=512):
    m = q.shape[0]
    t = m // batch
    tb = _tile(t, tb)
    nb = t // tb
    qi_tab, kj_tab = _pair_tables(nb, descending=True)
    grid_spec = pltpu.PrefetchScalarGridSpec(
        num_scalar_prefetch=2,
        grid=(batch, heads, int(qi_tab.shape[0])),
        in_specs=[pl.BlockSpec((tb, dh), lambda b, h, p, qt, kt: (b * nb + qt[p], h)),
                  pl.BlockSpec((tb, dh), lambda b, h, p, qt, kt: (b * nb + kt[p], h)),
                  pl.BlockSpec((tb, dh), lambda b, h, p, qt, kt: (b * nb + kt[p], h))],
        out_specs=pl.BlockSpec((tb, dh), lambda b, h, p, qt, kt: (b * nb + qt[p], h)),
        scratch_shapes=[pltpu.VMEM((tb, LANES), F32), pltpu.VMEM((tb, dh), F32)])
    return pl.pallas_call(
        _sb_kernel,
        grid_spec=grid_spec,
        out_shape=jax.ShapeDtypeStruct((m, heads * dh), BF16),
        compiler_params=_cparams(("parallel", "parallel", "arbitrary")),
        name="sb_attention",
    )(qi_tab, kj_tab, q, k, v)


def _lambda_value(lq1_ref, lk1_ref, lq2_ref, lk2_ref, lambda_init):
    a = jnp.sum(lq1_ref[...] * lk1_ref[...], axis=1, keepdims=True)
    b = jnp.sum(lq2_ref[...] * lk2_ref[...], axis=1, keepdims=True)
    return jnp.exp(a) - jnp.exp(b) + lambda_init


def _diff_kernel(qi_tab, kj_tab, slope_ref, q_ref, k_ref, v_ref, lq1_ref, lk1_ref, lq2_ref, lk2_ref,
                 g_ref, o_ref, m_s, l_s, acc_s, *, lambda_init, dq):
    h = pl.program_id(1)
    p = pl.program_id(2)
    qi, kj = qi_tab[p], kj_tab[p]
    t = q_ref.shape[0]
    slope = slope_ref[h]

    @pl.when(kj == 0)
    def _():
        m_s[...] = jnp.full(m_s.shape, -jnp.inf, F32)
        l_s[...] = jnp.zeros(l_s.shape, F32)
        acc_s[...] = jnp.zeros(acc_s.shape, F32)

    def step(diagonal):
        col_bias = slope * lax.broadcasted_iota(jnp.int32, (1, t), 1).astype(F32)
        row_pos = (qi - kj) * t + lax.broadcasted_iota(jnp.int32, (t, LANES), 0)
        row_shift = slope * row_pos.astype(F32)
        v = v_ref[...]
        for i in range(2):
            a = _dot_nt(q_ref[:, i * dq:(i + 1) * dq], k_ref[:, i * dq:(i + 1) * dq]) + col_bias
            if diagonal:
                r, c = _tri_iota(t)
                a = jnp.where(c <= r, a, NEG)
            m, l, acc = _softmax_step(a, row_shift, m_s[i], l_s[i], acc_s[i], v)
            m_s[i] = m
            l_s[i] = l
            acc_s[i] = acc

    @pl.when(kj < qi)
    def _():
        step(False)

    @pl.when(kj == qi)
    def _():
        step(True)
        lam = _lambda_value(lq1_ref, lk1_ref, lq2_ref, lk2_ref, lambda_init)
        reps = acc_s.shape[2] // LANES
        o = acc_s[0] / jnp.tile(l_s[0], (1, reps)) - lam * (acc_s[1] / jnp.tile(l_s[1], (1, reps)))
        o = o * lax.rsqrt(jnp.mean(o * o, axis=1, keepdims=True) + LN_EPS) * g_ref[...]
        o_ref[...] = (o * (1.0 - lambda_init)).astype(o_ref.dtype)


def _diff_attention(q, k, v, slopes, lq1, lk1, lq2, lk2, g, batch, heads, dq, lambda_init, tb=512):
    m = q.shape[0]
    t = m // batch
    tb = _tile(t, tb)
    nb = t // tb
    dv = 2 * dq
    qi_tab, kj_tab = _pair_tables(nb, descending=False)
    vec = pl.BlockSpec((1, dq), lambda b, h, p, qt, kt, sl: (0, 0))
    grid_spec = pltpu.PrefetchScalarGridSpec(
        num_scalar_prefetch=3,
        grid=(batch, heads, int(qi_tab.shape[0])),
        in_specs=[pl.BlockSpec((tb, dv), lambda b, h, p, qt, kt, sl: (b * nb + qt[p], h)),
                  pl.BlockSpec((tb, dv), lambda b, h, p, qt, kt, sl: (b * nb + kt[p], h)),
                  pl.BlockSpec((tb, dv), lambda b, h, p, qt, kt, sl: (b * nb + kt[p], h)),
                  vec, vec, vec, vec,
                  pl.BlockSpec((1, dv), lambda b, h, p, qt, kt, sl: (0, 0))],
        out_specs=pl.BlockSpec((tb, dv), lambda b, h, p, qt, kt, sl: (b * nb + qt[p], h)),
        scratch_shapes=[pltpu.VMEM((2, tb, LANES), F32), pltpu.VMEM((2, tb, LANES), F32),
                        pltpu.VMEM((2, tb, dv), F32)])
    return pl.pallas_call(
        functools.partial(_diff_kernel, lambda_init=lambda_init, dq=dq),
        grid_spec=grid_spec,
        out_shape=jax.ShapeDtypeStruct((m, heads * dv), BF16),
        compiler_params=_cparams(("parallel", "parallel", "arbitrary")),
        name="diff_attention",
    )(qi_tab, kj_tab, slopes, q, k, v, lq1, lk1, lq2, lk2, g)


def _own_head(rows, width, heads):
    assert heads & (heads - 1) == 0
    r = lax.broadcasted_iota(jnp.int32, (rows, width), 0)
    c = lax.broadcasted_iota(jnp.int32, (rows, width), 1)
    return (c & (heads - 1)) == (r & (heads - 1))


def _flat_pages(refs):
    return [ref[...].reshape(ref.shape[0] * ref.shape[1], ref.shape[2]).astype(BF16) for ref in refs]


def _wide_scores(q, k_refs):
    s = [_dot_nt(q, k) for k in _flat_pages(k_refs)]
    return jnp.concatenate(s, axis=1) if len(s) > 1 else s[0]


def _wide_pv(w, v_refs):
    out = None
    width = v_refs[0].shape[0] * v_refs[0].shape[1]
    for g, v in enumerate(_flat_pages(v_refs)):
        d = _dot(w[:, g * width:(g + 1) * width], v)
        out = d if out is None else out + d
    return out


def _wide_suffix_sums(x, carry, terms):
    rows, w = x.shape
    ch = min(CUM_CHUNK, w)
    n = w // ch
    u = _suffix_matrix(ch)
    parts = _bf16_terms(x, terms)
    stacked = jnp.concatenate([p[:, c * ch:(c + 1) * ch] for c in range(n) for p in parts], axis=0)
    d = _dot(stacked.astype(BF16), u)
    outs = [None] * n
    for c in reversed(range(n)):
        base = c * terms * rows
        tail = d[base:base + rows]
        for t in range(1, terms):
            tail = tail + d[base + t * rows:base + (t + 1) * rows]
        outs[c] = tail + carry
        carry = carry + jnp.sum(x[:, c * ch:(c + 1) * ch], axis=1, keepdims=True)
    return (jnp.concatenate(outs, axis=1) if n > 1 else outs[0]), carry


def _fox_dec_kernel(pt_ref, q_ref, kn_ref, vn_ref, lfn_ref, *refs, g_pages):
    lf_refs = refs[:g_pages]
    k_refs = refs[g_pages:2 * g_pages]
    v_refs = refs[2 * g_pages:3 * g_pages]
    o_ref, m_s, l_s, acc_s, carry_s = refs[3 * g_pages:]
    j = pl.program_id(1)
    heads = q_ref.shape[0]
    q = q_ref[...].astype(BF16)

    @pl.when(j == 0)
    def _():
        qn = q.astype(F32) * kn_ref[...].astype(BF16).astype(F32)
        m_s[...] = jnp.sum(qn, axis=1, keepdims=True)
        l_s[...] = jnp.ones(l_s.shape, F32)
        acc_s[...] = vn_ref[...].astype(BF16).astype(F32)
        carry_s[...] = lfn_ref[...]

    s = _wide_scores(q, k_refs)
    own = _own_head(heads, s.shape[1], heads)
    lf = jnp.concatenate([r[...] for r in lf_refs], axis=1) if g_pages > 1 else lf_refs[0][...]
    cum, carry = _wide_suffix_sums(jnp.where(own, lf, 0.0), carry_s[...], 3)
    carry_s[...] = carry
    s = jnp.where(own, s + cum, NEG)
    m_old = m_s[...]
    m_new = jnp.maximum(m_old, jnp.max(s, axis=1, keepdims=True))
    alpha = jnp.exp(m_old - m_new)
    pr = jnp.exp(s - m_new)
    l_s[...] = alpha * l_s[...] + jnp.sum(pr, axis=1, keepdims=True)
    acc_s[...] = alpha * acc_s[...] + _wide_pv(pr.astype(BF16), v_refs)
    m_s[...] = m_new

    @pl.when(j == pl.num_programs(1) - 1)
    def _():
        o_ref[...] = (acc_s[...] / l_s[...]).astype(o_ref.dtype)


def _sb_dec_kernel(pt_ref, q_ref, *refs, g_pages):
    k_refs = refs[:g_pages]
    v_refs = refs[g_pages:2 * g_pages]
    o_ref, acc_s, carry_s = refs[2 * g_pages:]
    j = pl.program_id(1)
    heads = q_ref.shape[0]
    q = q_ref[...].astype(BF16)

    @pl.when(j == 0)
    def _():
        acc_s[...] = jnp.zeros(acc_s.shape, F32)
        carry_s[...] = jnp.zeros(carry_s.shape, F32)

    z = _wide_scores(q, k_refs)
    own = _own_head(heads, z.shape[1], heads)
    log_keep = jnp.where(own, _log_sigmoid(-z), 0.0)
    tail, carry = _wide_suffix_sums(log_keep, carry_s[...], 2)
    carry_s[...] = carry
    a = jnp.where(own, jnp.exp(log_keep + z + tail), 0.0)
    acc_s[...] += _wide_pv(a.astype(BF16), v_refs)

    @pl.when(j == pl.num_programs(1) - 1)
    def _():
        o_ref[...] = acc_s[...].astype(o_ref.dtype)


def _diff_dec_kernel(pt_ref, q_ref, kn_ref, vn_ref, slope_ref, lq1_ref, lk1_ref, lq2_ref, lk2_ref,
                     g_ref, *refs, g_pages, lambda_init, n_pages):
    k_refs = refs[:g_pages]
    v_refs = refs[g_pages:2 * g_pages]
    o_ref, m_s, l_s, acc_s = refs[2 * g_pages:]
    j = pl.program_id(1)
    page, heads, _ = k_refs[0].shape
    rows = 2 * heads
    q = q_ref[...].astype(BF16)

    @pl.when(j == 0)
    def _():
        kn = kn_ref[...].astype(BF16).astype(F32)
        qn = q.astype(F32) * jnp.concatenate([kn, kn], axis=0)
        m_s[...] = jnp.sum(qn, axis=1, keepdims=True)
        l_s[...] = jnp.ones(l_s.shape, F32)
        vn = vn_ref[...].astype(BF16).astype(F32)
        acc_s[...] = jnp.concatenate([vn, vn], axis=0)

    s = _wide_scores(q, k_refs)
    own = _own_head(rows, s.shape[1], heads)
    first = (n_pages - (j + 1) * g_pages) * page
    col = lax.broadcasted_iota(jnp.int32, s.shape, 1)
    dist = (n_pages * page - first - jnp.right_shift(col, heads.bit_length() - 1)).astype(F32)
    s = jnp.where(own, s - slope_ref[...] * dist, NEG)
    m_old = m_s[...]
    m_new = jnp.maximum(m_old, jnp.max(s, axis=1, keepdims=True))
    alpha = jnp.exp(m_old - m_new)
    pr = jnp.exp(s - m_new)
    l_s[...] = alpha * l_s[...] + jnp.sum(pr, axis=1, keepdims=True)
    acc_s[...] = alpha * acc_s[...] + _wide_pv(pr.astype(BF16), v_refs)
    m_s[...] = m_new

    @pl.when(j == pl.num_programs(1) - 1)
    def _():
        lam = _lambda_value(lq1_ref, lk1_ref, lq2_ref, lk2_ref, lambda_init)
        on = acc_s[...] / l_s[...]
        o = on[:heads] - lam * on[heads:]
        o = o * lax.rsqrt(jnp.mean(o * o, axis=1, keepdims=True) + LN_EPS) * g_ref[...]
        o_ref[...] = (o * (1.0 - lambda_init)).astype(o_ref.dtype)


def _page_specs(n_pages, g_pages, block, layer):
    def make(g):
        def index(b, j, pt):
            return (layer, pt[b, n_pages - (j + 1) * g_pages + g]) + (0,) * len(block)
        return pl.BlockSpec((None, None) + block, index)
    return [make(g) for g in range(g_pages)]


def _per_seq_spec(shape):
    return pl.BlockSpec((None,) + shape, lambda b, j, pt: (b,) + (0,) * len(shape))


def _shared_spec(shape):
    return pl.BlockSpec(shape, lambda b, j, pt: (0,) * len(shape))


def _fox_decode(page_table, q, kn, vn, lfn, lf_cache, k_cache, v_cache, layer, g_pages=8):
    bsz, n_pages = page_table.shape
    _, _, page, heads, dh = k_cache.shape
    g_pages = _tile(n_pages, g_pages)
    grid_spec = pltpu.PrefetchScalarGridSpec(
        num_scalar_prefetch=1,
        grid=(bsz, n_pages // g_pages),
        in_specs=([_per_seq_spec((heads, dh))] * 3 + [_per_seq_spec((heads, 1))]
                  + _page_specs(n_pages, g_pages, (1, page * heads), layer)
                  + _page_specs(n_pages, g_pages, (page, heads, dh), layer)
                  + _page_specs(n_pages, g_pages, (page, heads, dh), layer)),
        out_specs=_per_seq_spec((heads, dh)),
        scratch_shapes=[pltpu.VMEM((heads, 1), F32), pltpu.VMEM((heads, 1), F32),
                        pltpu.VMEM((heads, dh), F32), pltpu.VMEM((heads, 1), F32)])
    return pl.pallas_call(
        functools.partial(_fox_dec_kernel, g_pages=g_pages),
        grid_spec=grid_spec,
        out_shape=jax.ShapeDtypeStruct((bsz, heads, dh), BF16),
        compiler_params=_cparams(("parallel", "arbitrary")),
        name="fox_decode",
    )(page_table, q, kn, vn, lfn, *([lf_cache] * g_pages), *([k_cache] * g_pages), *([v_cache] * g_pages))


def _sb_decode(page_table, q, k_cache, v_cache, layer, g_pages=8):
    bsz, n_pages = page_table.shape
    _, _, page, heads, dh = k_cache.shape
    g_pages = _tile(n_pages, g_pages)
    grid_spec = pltpu.PrefetchScalarGridSpec(
        num_scalar_prefetch=1,
        grid=(bsz, n_pages // g_pages),
        in_specs=([_per_seq_spec((heads, dh))]
                  + _page_specs(n_pages, g_pages, (page, heads, dh), layer)
                  + _page_specs(n_pages, g_pages, (page, heads, dh), layer)),
        out_specs=_per_seq_spec((heads, dh)),
        scratch_shapes=[pltpu.VMEM((heads, dh), F32), pltpu.VMEM((heads, 1), F32)])
    return pl.pallas_call(
        functools.partial(_sb_dec_kernel, g_pages=g_pages),
        grid_spec=grid_spec,
        out_shape=jax.ShapeDtypeStruct((bsz, heads, dh), BF16),
        compiler_params=_cparams(("parallel", "arbitrary")),
        name="sb_decode",
    )(page_table, q, *([k_cache] * g_pages), *([v_cache] * g_pages))


def _diff_decode(page_table, q, kn, vn, slopes, lq1, lk1, lq2, lk2, g, k_cache, v_cache, layer, lambda_init,
                 g_pages=4):
    bsz, n_pages = page_table.shape
    _, _, page, heads, dv = k_cache.shape
    dq = dv // 2
    rows = 2 * heads
    g_pages = _tile(n_pages, g_pages)
    grid_spec = pltpu.PrefetchScalarGridSpec(
        num_scalar_prefetch=1,
        grid=(bsz, n_pages // g_pages),
        in_specs=([_per_seq_spec((rows, dv)), _per_seq_spec((heads, dv)), _per_seq_spec((heads, dv)),
                   _shared_spec((rows, 1)),
                   _shared_spec((1, dq)), _shared_spec((1, dq)), _shared_spec((1, dq)), _shared_spec((1, dq)),
                   _shared_spec((1, dv))]
                  + _page_specs(n_pages, g_pages, (page, heads, dv), layer)
                  + _page_specs(n_pages, g_pages, (page, heads, dv), layer)),
        out_specs=_per_seq_spec((heads, dv)),
        scratch_shapes=[pltpu.VMEM((rows, 1), F32), pltpu.VMEM((rows, 1), F32),
                        pltpu.VMEM((rows, dv), F32)])
    return pl.pallas_call(
        functools.partial(_diff_dec_kernel, g_pages=g_pages, lambda_init=lambda_init, n_pages=n_pages),
        grid_spec=grid_spec,
        out_shape=jax.ShapeDtypeStruct((bsz, heads, dv), BF16),
        compiler_params=_cparams(("parallel", "arbitrary")),
        name="diff_decode",
    )(page_table, q, kn, vn, slopes, lq1, lk1, lq2, lk2, g, *([k_cache] * g_pages), *([v_cache] * g_pages))


def _even_layer(xpb, xsb, caches, page_table, w_in, b_f, w_o, dims):
    cache_k_a, cache_v_a, cache_lf, cache_k_b, cache_v_b, layer_idx = caches
    bsz, t, dbs, heads, dh = dims
    w = heads * dh
    scale = dh ** -0.5
    wb = w_in.astype(BF16)
    seg = [wb[:, i * w:(i + 1) * w] for i in range(3)] + [wb[:, 3 * w + heads + i * w:3 * w + heads + (i + 1) * w]
                                                         for i in range(3)]
    w_f = jnp.pad(wb[:, 3 * w:3 * w + heads], ((0, 0), (0, LANES - heads)))
    b_pad = jnp.pad(b_f.astype(F32), (0, LANES - heads)).reshape(1, LANES)
    wo_a, wo_b = w_o[:w].astype(BF16), w_o[w:].astype(BF16)

    def project(xb, q_dtype):
        (q_a,) = _mm(xb, seg[0], (q_dtype,), out_scale=scale)
        k_a, k_ab = _mm(xb, seg[1], (F32, BF16))
        v_a, v_ab = _mm(xb, seg[2], (F32, BF16))
        (q_b,) = _mm(xb, seg[3], (q_dtype,), out_scale=scale)
        k_b, k_bb = _mm(xb, seg[4], (F32, BF16))
        v_b, v_bb = _mm(xb, seg[5], (F32, BF16))
        lf = _forget_gate(xb, w_f, b_pad)[:, :heads]
        return (q_a, k_a, k_ab, v_a, v_ab, q_b, k_b, k_bb, v_b, v_bb, lf)

    q_a, k_a, k_ab, v_a, v_ab, q_b, k_b, k_bb, v_b, v_bb, lf = project(xpb, BF16)
    lf_t = jnp.swapaxes(lf.reshape(bsz, t, heads), 1, 2)
    cum = _suffix_sums(lf_t.reshape(bsz * heads, t)).reshape(bsz, heads, 1, t)
    o_a = _fox_attention(q_a, k_ab, v_ab, cum, bsz, heads, dh)
    o_b = _sb_attention(q_b, k_bb, v_bb, bsz, heads, dh)
    new_p = (k_a.reshape(bsz, t, heads, dh), v_a.reshape(bsz, t, heads, dh), lf.reshape(bsz, t, heads),
             k_b.reshape(bsz, t, heads, dh), v_b.reshape(bsz, t, heads, dh))

    sq_a, sk_a, _, sv_a, _, sq_b, sk_b, _, sv_b, _, slf = project(xsb, F32)
    per_head = lambda a: a.reshape(dbs, heads, dh)
    n_l, n_pool, page, _ = cache_lf.shape
    lf_flat = cache_lf.reshape(n_l, n_pool, 1, page * heads)
    so_a = _fox_decode(page_table, per_head(sq_a), per_head(sk_a), per_head(sv_a), slf.reshape(dbs, heads, 1),
                       lf_flat, cache_k_a, cache_v_a, layer_idx)
    so_b = _sb_decode(page_table, per_head(sq_b), cache_k_b, cache_v_b, layer_idx)
    new_s = (sk_a.reshape(dbs, 1, heads, dh), sv_a.reshape(dbs, 1, heads, dh), slf.reshape(dbs, 1, heads),
             sk_b.reshape(dbs, 1, heads, dh), sv_b.reshape(dbs, 1, heads, dh))
    return (o_a, o_b), (so_a.reshape(dbs, w), so_b.reshape(dbs, w)), (wo_a, wo_b), new_p, new_s


def _odd_layer(xpb, xsb, caches, page_table, w_in, lq1, lk1, lq2, lk2, gain, w_o, lambda_init, dims):
    cache_k, cache_v, layer_idx = caches
    bsz, t, dbs, heads, dq = dims
    dv = 2 * dq
    w = heads * dv
    scale = dq ** -0.5
    wb = w_in.astype(BF16)
    seg = [wb[:, i * w:(i + 1) * w] for i in range(3)]
    slopes = 2.0 ** (-8.0 * jnp.arange(1, heads + 1, dtype=F32) / heads)
    vecs = [a.astype(F32).reshape(1, dq) for a in (lq1, lk1, lq2, lk2)]
    g_row = gain.astype(F32).reshape(1, dv)

    def project(xb, q_dtype):
        (q,) = _mm(xb, seg[0], (q_dtype,), out_scale=scale)
        k, kb = _mm(xb, seg[1], (F32, BF16))
        v, vb = _mm(xb, seg[2], (F32, BF16))
        return q, k, kb, v, vb

    q, k, kb, v, vb = project(xpb, BF16)
    o = _diff_attention(q, kb, vb, slopes, *vecs, g_row, bsz, heads, dq, lambda_init)
    new_p = (k.reshape(bsz, t, heads, dv), v.reshape(bsz, t, heads, dv))

    sq, sk, _, sv, _ = project(xsb, F32)
    sq4 = sq.reshape(dbs, heads, 2, dq)
    zero = jnp.zeros((dbs, heads, dq), F32)
    q_rows = jnp.concatenate([jnp.concatenate([sq4[:, :, 0], zero], axis=2),
                              jnp.concatenate([zero, sq4[:, :, 1]], axis=2)], axis=1)
    slope_rows = jnp.concatenate([slopes, slopes]).reshape(2 * heads, 1)
    so = _diff_decode(page_table, q_rows, sk.reshape(dbs, heads, dv), sv.reshape(dbs, heads, dv), slope_rows,
                      *vecs, g_row, cache_k, cache_v, layer_idx, lambda_init)
    new_s = (sk.reshape(dbs, 1, heads, dv), sv.reshape(dbs, 1, heads, dv))
    return (o,), (so.reshape(dbs, w),), (w_o.astype(BF16),), new_p, new_s


def kernel(x_prompt, x_sample, cache_fox_k, cache_fox_v, cache_fox_logf, cache_sb_k, cache_sb_v, cache_diff_k, cache_diff_v, page_table, w_in_even, b_forget, w_out_even, w_in_odd, lambda_q1, lambda_k1, lambda_q2, lambda_k2, subln_gain, w_out_odd, ln_mix_gain, ln_mix_bias, ln_ffn_gain, ln_ffn_bias, w_ffn_up, w_ffn_down):
    bsz, t, d = x_prompt.shape
    dbs, ts, _ = x_sample.shape
    assert ts == 1, "the sample group carries one new token per sequence"
    depth = ln_mix_gain.shape[0]
    alpha = (2 * depth) ** 0.25
    h_fox, dh = cache_fox_k.shape[3], cache_fox_k.shape[4]
    h_diff, dq = cache_diff_k.shape[3], cache_diff_k.shape[4] // 2
    assert cache_sb_k.shape[3:] == (h_fox, dh), "both head groups of an even layer share one geometry"

    xp = x_prompt.reshape(bsz * t, d)
    xs = x_sample.reshape(dbs, d)
    xpb, xsb = xp.astype(BF16), xs.astype(BF16)
    even_p, even_s, odd_p, odd_s = [], [], [], []
    row = lambda a: a.astype(F32).reshape(1, d)
    for layer in range(depth):
        i = layer // 2
        if layer % 2 == 0:
            caches = (cache_fox_k, cache_fox_v, cache_fox_logf, cache_sb_k, cache_sb_v, i)
            o_p, o_s, w_o, new_p, new_s = _even_layer(xpb, xsb, caches, page_table, w_in_even[i],
                                                      b_forget[i], w_out_even[i], (bsz, t, dbs, h_fox, dh))
            even_p.append(new_p)
            even_s.append(new_s)
        else:
            lambda_init = 0.8 - 0.6 * math.exp(-0.3 * layer)
            caches = (cache_diff_k, cache_diff_v, i)
            o_p, o_s, w_o, new_p, new_s = _odd_layer(xpb, xsb, caches, page_table, w_in_odd[i], lambda_q1[i],
                                                     lambda_k1[i], lambda_q2[i], lambda_k2[i], subln_gain[i],
                                                     w_out_odd[i], lambda_init, (bsz, t, dbs, h_diff, dq))
            odd_p.append(new_p)
            odd_s.append(new_s)
        g1, b1 = row(ln_mix_gain[layer]), row(ln_mix_bias[layer])
        g2, b2 = row(ln_ffn_gain[layer]), row(ln_ffn_bias[layer])
        w_up, w_down = w_ffn_up[layer].astype(BF16), w_ffn_down[layer].astype(BF16)
        xp, xpb = _proj_residual_ln(list(o_p), list(w_o), xp, g1, b1, alpha)
        xs, xsb = _proj_residual_ln(list(o_s), list(w_o), xs, g1, b1, alpha)
        xp, xpb = _ffn_residual_ln(xpb, xp, w_up, w_down, g2, b2, alpha)
        xs, xsb = _ffn_residual_ln(xsb, xs, w_up, w_down, g2, b2, alpha)

    def stk(rows, j):
        return jnp.stack([r[j] for r in rows])

    return (xp.reshape(bsz, t, d), xs.reshape(dbs, 1, d),
            stk(even_p, 0), stk(even_p, 1), stk(even_p, 2), stk(even_p, 3), stk(even_p, 4), stk(odd_p, 0), stk(odd_p, 1),
            stk(even_s, 0), stk(even_s, 1), stk(even_s, 2), stk(even_s, 3), stk(even_s, 4), stk(odd_s, 0), stk(odd_s, 1))
```

```python
import functools
import math

import jax
import jax.numpy as jnp
from jax import lax
from jax.experimental import pallas as pl
from jax.experimental.pallas import tpu as pltpu

F32 = jnp.float32
BF16 = jnp.bfloat16
LN_EPS = 1e-5
NEG = -1e30
LOG2E = 1.4426950408889634
LANES = 128
CUM_CHUNK = 256
DIAG_STRIP = 256
VMEM_LIMIT = 56 * 1024 * 1024


def _cparams(sem):
    return pltpu.CompilerParams(dimension_semantics=sem, vmem_limit_bytes=VMEM_LIMIT)


def _tile(n, pref):
    t = min(n, pref)
    while n % t:
        t //= 2
    return t


def _dot(a, b):
    return jnp.dot(a, b, preferred_element_type=F32)


def _dot_nt(a, b):
    return lax.dot_general(a, b, (((1,), (1,)), ((), ())), preferred_element_type=F32)


def _bf16_terms(x, terms):
    out = []
    r = x
    for t in range(terms):
        h = r.astype(BF16).astype(F32)
        out.append(h)
        if t + 1 < terms:
            r = r - h
    return out


def _split_dot(x, u, terms):
    out = None
    for h in _bf16_terms(x, terms):
        d = _dot(h.astype(BF16), u)
        out = d if out is None else out + d
    return out


def _suffix_matrix(n):
    r = lax.broadcasted_iota(jnp.int32, (n, n), 0)
    c = lax.broadcasted_iota(jnp.int32, (n, n), 1)
    return jnp.where(r > c, 1.0, 0.0).astype(BF16)


def _log_sigmoid(x):
    return jnp.minimum(x, 0.0) - jnp.log1p(jnp.exp(-jnp.abs(x)))


def _layer_norm(x, g, b):
    mu = jnp.mean(x, axis=-1, keepdims=True)
    xc = x - mu
    var = jnp.mean(xc * xc, axis=-1, keepdims=True)
    return xc * lax.rsqrt(var + LN_EPS) * g + b


def _mm_kernel(x_ref, w_ref, *o_refs, out_scale):
    acc = _dot(x_ref[...], w_ref[...])
    if out_scale != 1.0:
        acc = acc * out_scale
    for o_ref in o_refs:
        o_ref[...] = acc.astype(o_ref.dtype)


def _mm(x, w, out_dtypes, out_scale=1.0, tm=1024, tn=1024):
    m, k = x.shape
    n = w.shape[1]
    tm, tn = _tile(m, tm), _tile(n, tn)
    outs = pl.pallas_call(
        functools.partial(_mm_kernel, out_scale=out_scale),
        grid=(n // tn, m // tm),
        in_specs=[pl.BlockSpec((tm, k), lambda j, i: (i, 0)),
                  pl.BlockSpec((k, tn), lambda j, i: (0, j))],
        out_specs=[pl.BlockSpec((tm, tn), lambda j, i: (i, j)) for _ in out_dtypes],
        out_shape=[jax.ShapeDtypeStruct((m, n), d) for d in out_dtypes],
        compiler_params=_cparams(("parallel", "parallel")),
        name="proj",
    )(x, w)
    return outs


def _gate_kernel(x_ref, w_ref, b_ref, o_ref):
    o_ref[...] = _log_sigmoid(_dot(x_ref[...], w_ref[...]) + b_ref[...])


def _forget_gate(x, w, b, tm=512):
    m, k = x.shape
    n = w.shape[1]
    tm = _tile(m, tm)
    return pl.pallas_call(
        _gate_kernel,
        grid=(m // tm,),
        in_specs=[pl.BlockSpec((tm, k), lambda i: (i, 0)),
                  pl.BlockSpec((k, n), lambda i: (0, 0)),
                  pl.BlockSpec((1, n), lambda i: (0, 0))],
        out_specs=pl.BlockSpec((tm, n), lambda i: (i, 0)),
        out_shape=jax.ShapeDtypeStruct((m, n), F32),
        compiler_params=_cparams(("parallel",)),
        name="forget_gate",
    )(x, w, b)


def _revcum_kernel(x_ref, o_ref, *, out_scale):
    rows, t = x_ref.shape
    ch = min(CUM_CHUNK, t)
    u = _suffix_matrix(ch)
    carry = jnp.zeros((rows, 1), F32)
    for c in reversed(range(t // ch)):
        xc = x_ref[:, c * ch:(c + 1) * ch]
        o_ref[:, c * ch:(c + 1) * ch] = (_split_dot(xc, u, 3) + carry) * out_scale
        carry = carry + jnp.sum(xc, axis=1, keepdims=True)


def _suffix_sums(x, out_scale):
    return pl.pallas_call(
        functools.partial(_revcum_kernel, out_scale=out_scale),
        out_shape=jax.ShapeDtypeStruct(x.shape, F32),
        name="suffix_sums",
    )(x)


def _proj_ln_kernel(*refs, n_pairs, alpha):
    a_refs = refs[:n_pairs]
    w_refs = refs[n_pairs:2 * n_pairs]
    x_ref, g_ref, b_ref, y_ref, yb_ref = refs[2 * n_pairs:]
    acc = alpha * x_ref[...]
    for a_ref, w_ref in zip(a_refs, w_refs):
        acc = acc + _dot(a_ref[...], w_ref[...])
    y = _layer_norm(acc, g_ref[...], b_ref[...])
    y_ref[...] = y
    yb_ref[...] = y.astype(BF16)


def _proj_residual_ln(a_list, w_list, x, g, b, alpha, tm=256):
    m, d = x.shape
    tm = _tile(m, tm)
    n_pairs = len(a_list)
    in_specs = ([pl.BlockSpec((tm, a.shape[1]), lambda i: (i, 0)) for a in a_list]
                + [pl.BlockSpec(w.shape, lambda i: (0, 0)) for w in w_list]
                + [pl.BlockSpec((tm, d), lambda i: (i, 0)),
                   pl.BlockSpec((1, d), lambda i: (0, 0)),
                   pl.BlockSpec((1, d), lambda i: (0, 0))])
    return pl.pallas_call(
        functools.partial(_proj_ln_kernel, n_pairs=n_pairs, alpha=alpha),
        grid=(m // tm,),
        in_specs=in_specs,
        out_specs=[pl.BlockSpec((tm, d), lambda i: (i, 0)), pl.BlockSpec((tm, d), lambda i: (i, 0))],
        out_shape=[jax.ShapeDtypeStruct((m, d), F32), jax.ShapeDtypeStruct((m, d), BF16)],
        compiler_params=_cparams(("parallel",)),
        name="out_proj_ln",
    )(*a_list, *w_list, x, g, b)


def _ffn_kernel(xb_ref, x_ref, wu_ref, wd_ref, g_ref, b_ref, y_ref, yb_ref, acc_ref, *, alpha):
    f = pl.program_id(1)

    @pl.when(f == 0)
    def _():
        acc_ref[...] = alpha * x_ref[...]

    h = jnp.maximum(_dot(xb_ref[...], wu_ref[...]), 0.0)
    acc_ref[...] += _dot((h * h).astype(BF16), wd_ref[...])

    @pl.when(f == pl.num_programs(1) - 1)
    def _():
        y = _layer_norm(acc_ref[...], g_ref[...], b_ref[...])
        y_ref[...] = y
        yb_ref[...] = y.astype(BF16)


def _ffn_residual_ln(xb, x, w_up, w_down, g, b, alpha, tm=512, tf=1024):
    m, d = x.shape
    dff = w_up.shape[1]
    tm, tf = _tile(m, tm), _tile(dff, tf)
    return pl.pallas_call(
        functools.partial(_ffn_kernel, alpha=alpha),
        grid=(m // tm, dff // tf),
        in_specs=[pl.BlockSpec((tm, d), lambda i, f: (i, 0)),
                  pl.BlockSpec((tm, d), lambda i, f: (i, 0)),
                  pl.BlockSpec((d, tf), lambda i, f: (0, f)),
                  pl.BlockSpec((tf, d), lambda i, f: (f, 0)),
                  pl.BlockSpec((1, d), lambda i, f: (0, 0)),
                  pl.BlockSpec((1, d), lambda i, f: (0, 0))],
        out_specs=[pl.BlockSpec((tm, d), lambda i, f: (i, 0)), pl.BlockSpec((tm, d), lambda i, f: (i, 0))],
        out_shape=[jax.ShapeDtypeStruct((m, d), F32), jax.ShapeDtypeStruct((m, d), BF16)],
        scratch_shapes=[pltpu.VMEM((tm, d), F32)],
        compiler_params=_cparams(("parallel", "arbitrary")),
        name="ffn_ln",
    )(xb, x, w_up, w_down, g, b)


def _pair_tables(n_blocks, descending):
    qi, kj = [], []
    for q in range(n_blocks):
        ks = range(q, -1, -1) if descending else range(q + 1)
        for k in ks:
            qi.append(q)
            kj.append(k)
    return jnp.asarray(qi, jnp.int32), jnp.asarray(kj, jnp.int32)


def _tri_iota(t):
    r = lax.broadcasted_iota(jnp.int32, (t, t), 0)
    c = lax.broadcasted_iota(jnp.int32, (t, t), 1)
    return r, c


def _causal(rows, width, row0=0, col0=0, strict=False):
    r = jnp.tile(lax.broadcasted_iota(jnp.int32, (rows, LANES), 0), (1, width // LANES)) + row0
    c = lax.broadcasted_iota(jnp.int32, (1, width), 1) + col0
    return (c < r) if strict else (c <= r)


def _row_strips(t, diagonal, rows=DIAG_STRIP):
    if not diagonal:
        return [(0, t, t)]
    rows = min(rows, t)
    return [(r0, r0 + rows, r0 + rows) for r0 in range(0, t, rows)]


def _lanes_to_rows(row, t):
    r, c = _tri_iota(t)
    col = jnp.sum(jnp.where(r == c, row, 0.0), axis=1, keepdims=True)
    return jnp.broadcast_to(col, (t, LANES))


def _softmax_step(a, row_shift, m_prev, l_prev, acc_prev, v):
    t, tk = a.shape
    m_next = jnp.maximum(m_prev, jnp.max(a, axis=1, keepdims=True) - row_shift)
    pr = jnp.exp2(a - jnp.tile(m_next + row_shift, (1, tk // LANES)))
    alpha = jnp.exp2(m_prev - m_next)
    l_next = alpha * l_prev + jnp.sum(pr, axis=1, keepdims=True)
    acc_next = jnp.tile(alpha, (1, v.shape[1] // LANES)) * acc_prev + _dot(pr.astype(BF16), v)
    return m_next, l_next, acc_next


def _fox_kernel(qi_tab, kj_tab, q_ref, k_ref, v_ref, ck_ref, cqrow_ref, o_ref, m_s, l_s, acc_s, cq_s, *, dh):
    p = pl.program_id(2)
    qi, kj = qi_tab[p], kj_tab[p]
    t = q_ref.shape[0]
    hb = m_s.shape[0]

    @pl.when(kj == 0)
    def _():
        m_s[...] = jnp.full(m_s.shape, -jnp.inf, F32)
        l_s[...] = jnp.zeros(l_s.shape, F32)
        acc_s[...] = jnp.zeros(acc_s.shape, F32)
        for i in range(hb):
            cq_s[i] = _lanes_to_rows(cqrow_ref[i], t)

    def step(diagonal):
        for i in range(hb):
            cols = slice(i * dh, (i + 1) * dh)
            for r0, r1, kc in _row_strips(t, diagonal):
                a = _dot_nt(q_ref[r0:r1, cols], k_ref[:kc, cols]) + ck_ref[i, :, :kc]
                if diagonal:
                    a = jnp.where(_causal(r1 - r0, kc, row0=r0), a, NEG)
                m, l, acc = _softmax_step(a, cq_s[i, r0:r1], m_s[i, r0:r1], l_s[i, r0:r1], acc_s[i, r0:r1],
                                          v_ref[:kc, cols])
                m_s[i, r0:r1] = m
                l_s[i, r0:r1] = l
                acc_s[i, r0:r1] = acc

    @pl.when(kj < qi)
    def _():
        step(False)

    @pl.when(kj == qi)
    def _():
        step(True)
        for i in range(hb):
            o_ref[:, i * dh:(i + 1) * dh] = (acc_s[i] / l_s[i]).astype(o_ref.dtype)


def _fox_attention(q, k, v, cum, batch, heads, dh, tb=512, hb=2):
    assert dh == LANES
    m = q.shape[0]
    t = m // batch
    tb = _tile(t, tb)
    hb = _tile(heads, hb)
    nb = t // tb
    qi_tab, kj_tab = _pair_tables(nb, descending=False)
    grid_spec = pltpu.PrefetchScalarGridSpec(
        num_scalar_prefetch=2,
        grid=(batch, heads // hb, int(qi_tab.shape[0])),
        in_specs=[pl.BlockSpec((tb, hb * dh), lambda b, h, p, qt, kt: (b * nb + qt[p], h)),
                  pl.BlockSpec((tb, hb * dh), lambda b, h, p, qt, kt: (b * nb + kt[p], h)),
                  pl.BlockSpec((tb, hb * dh), lambda b, h, p, qt, kt: (b * nb + kt[p], h)),
                  pl.BlockSpec((None, hb, 1, tb), lambda b, h, p, qt, kt: (b, h, 0, kt[p])),
                  pl.BlockSpec((None, hb, 1, tb), lambda b, h, p, qt, kt: (b, h, 0, qt[p]))],
        out_specs=pl.BlockSpec((tb, hb * dh), lambda b, h, p, qt, kt: (b * nb + qt[p], h)),
        scratch_shapes=[pltpu.VMEM((hb, tb, LANES), F32), pltpu.VMEM((hb, tb, LANES), F32),
                        pltpu.VMEM((hb, tb, dh), F32), pltpu.VMEM((hb, tb, LANES), F32)])
    return pl.pallas_call(
        functools.partial(_fox_kernel, dh=dh),
        grid_spec=grid_spec,
        out_shape=jax.ShapeDtypeStruct((m, heads * dh), BF16),
        compiler_params=_cparams(("parallel", "parallel", "arbitrary")),
        name="fox_attention",
    )(qi_tab, kj_tab, q, k, v, cum, cum)


def _sb_kernel(qi_tab, kj_tab, q_ref, k_ref, v_ref, o_ref, carry_s, acc_s, *, dh):
    p = pl.program_id(2)
    qi, kj = qi_tab[p], kj_tab[p]
    t = q_ref.shape[0]
    ch = min(CUM_CHUNK, t)
    hb = acc_s.shape[0]

    @pl.when(kj == qi)
    def _():
        carry_s[...] = jnp.zeros(carry_s.shape, F32)
        acc_s[...] = jnp.zeros(acc_s.shape, F32)

    def step(diagonal):
        u = _suffix_matrix(ch)
        for i in range(hb):
            cols = slice(i * dh, (i + 1) * dh)
            for r0, r1, kc in _row_strips(t, diagonal, ch):
                drop = carry_s[i, r0:r1]
                acc = acc_s[i, r0:r1]
                for c0 in reversed(range(0, kc, ch)):
                    z = _dot_nt(q_ref[r0:r1, cols], k_ref[c0:c0 + ch, cols])
                    sp = jnp.maximum(z, 0.0) + jnp.log2(1.0 + jnp.exp2(-jnp.abs(z)))
                    on_diagonal = diagonal and c0 + ch > r0
                    if on_diagonal:
                        strict = _causal(r1 - r0, ch, row0=r0, col0=c0, strict=True)
                        sp = jnp.where(strict, sp, 0.0)
                    hi, lo = (x.astype(BF16) for x in _bf16_terms(sp, 2))
                    a = jnp.exp2(z - sp - (_dot(hi, u) + _dot(lo, u)) - jnp.tile(drop, (1, ch // LANES)))
                    if on_diagonal:
                        a = jnp.where(strict, a, 0.0)
                    acc = acc + _dot(a.astype(BF16), v_ref[c0:c0 + ch, cols])
                    drop = drop + jnp.sum(sp, axis=1, keepdims=True)
                carry_s[i, r0:r1] = drop
                acc_s[i, r0:r1] = acc

    @pl.when(kj == qi)
    def _():
        step(True)

    @pl.when(kj < qi)
    def _():
        step(False)

    @pl.when(kj == 0)
    def _():
        for i in range(hb):
            o_ref[:, i * dh:(i + 1) * dh] = acc_s[i].astype(o_ref.dtype)


def _sb_attention(q, k, v, batch, heads, dh, tb=512, hb=2):
    m = q.shape[0]
    t = m // batch
    tb = _tile(t, tb)
    hb = _tile(heads, hb)
    nb = t // tb
    qi_tab, kj_tab = _pair_tables(nb, descending=True)
    grid_spec = pltpu.PrefetchScalarGridSpec(
        num_scalar_prefetch=2,
        grid=(batch, heads // hb, int(qi_tab.shape[0])),
        in_specs=[pl.BlockSpec((tb, hb * dh), lambda b, h, p, qt, kt: (b * nb + qt[p], h)),
                  pl.BlockSpec((tb, hb * dh), lambda b, h, p, qt, kt: (b * nb + kt[p], h)),
                  pl.BlockSpec((tb, hb * dh), lambda b, h, p, qt, kt: (b * nb + kt[p], h))],
        out_specs=pl.BlockSpec((tb, hb * dh), lambda b, h, p, qt, kt: (b * nb + qt[p], h)),
        scratch_shapes=[pltpu.VMEM((hb, tb, LANES), F32), pltpu.VMEM((hb, tb, dh), F32)])
    return pl.pallas_call(
        functools.partial(_sb_kernel, dh=dh),
        grid_spec=grid_spec,
        out_shape=jax.ShapeDtypeStruct((m, heads * dh), BF16),
        compiler_params=_cparams(("parallel", "parallel", "arbitrary")),
        name="sb_attention",
    )(qi_tab, kj_tab, q, k, v)


def _lambda_value(lq1_ref, lk1_ref, lq2_ref, lk2_ref, lambda_init):
    a = jnp.sum(lq1_ref[...] * lk1_ref[...], axis=1, keepdims=True)
    b = jnp.sum(lq2_ref[...] * lk2_ref[...], axis=1, keepdims=True)
    return jnp.exp(a) - jnp.exp(b) + lambda_init


def _diff_kernel(qi_tab, kj_tab, slope_ref, q_ref, k_ref, v_ref, lq1_ref, lk1_ref, lq2_ref, lk2_ref,
                 g_ref, o_ref, m_s, l_s, acc_s, *, lambda_init, dq):
    h = pl.program_id(1)
    p = pl.program_id(2)
    qi, kj = qi_tab[p], kj_tab[p]
    t = q_ref.shape[0]
    slope = slope_ref[h]

    @pl.when(kj == 0)
    def _():
        m_s[...] = jnp.full(m_s.shape, -jnp.inf, F32)
        l_s[...] = jnp.zeros(l_s.shape, F32)
        acc_s[...] = jnp.zeros(acc_s.shape, F32)

    def step(diagonal):
        col_bias = slope * lax.broadcasted_iota(jnp.int32, (1, t), 1).astype(F32)
        row_pos = (qi - kj) * t + lax.broadcasted_iota(jnp.int32, (t, LANES), 0)
        row_shift = slope * row_pos.astype(F32)
        for i in range(2):
            cols = slice(i * dq, (i + 1) * dq)
            for r0, r1, kc in _row_strips(t, diagonal):
                a = _dot_nt(q_ref[r0:r1, cols], k_ref[:kc, cols]) + col_bias[:, :kc]
                if diagonal:
                    a = jnp.where(_causal(r1 - r0, kc, row0=r0), a, NEG)
                m, l, acc = _softmax_step(a, row_shift[r0:r1], m_s[i, r0:r1], l_s[i, r0:r1], acc_s[i, r0:r1],
                                          v_ref[:kc, :])
                m_s[i, r0:r1] = m
                l_s[i, r0:r1] = l
                acc_s[i, r0:r1] = acc

    @pl.when(kj < qi)
    def _():
        step(False)

    @pl.when(kj == qi)
    def _():
        step(True)
        lam = _lambda_value(lq1_ref, lk1_ref, lq2_ref, lk2_ref, lambda_init)
        reps = acc_s.shape[2] // LANES
        o = acc_s[0] / jnp.tile(l_s[0], (1, reps)) - lam * (acc_s[1] / jnp.tile(l_s[1], (1, reps)))
        o = o * lax.rsqrt(jnp.mean(o * o, axis=1, keepdims=True) + LN_EPS) * g_ref[...]
        o_ref[...] = (o * (1.0 - lambda_init)).astype(o_ref.dtype)


def _diff_attention(q, k, v, slopes, lq1, lk1, lq2, lk2, g, batch, heads, dq, lambda_init, tb=512):
    m = q.shape[0]
    t = m // batch
    tb = _tile(t, tb)
    nb = t // tb
    dv = 2 * dq
    qi_tab, kj_tab = _pair_tables(nb, descending=False)
    vec = pl.BlockSpec((1, dq), lambda b, h, p, qt, kt, sl: (0, 0))
    grid_spec = pltpu.PrefetchScalarGridSpec(
        num_scalar_prefetch=3,
        grid=(batch, heads, int(qi_tab.shape[0])),
        in_specs=[pl.BlockSpec((tb, dv), lambda b, h, p, qt, kt, sl: (b * nb + qt[p], h)),
                  pl.BlockSpec((tb, dv), lambda b, h, p, qt, kt, sl: (b * nb + kt[p], h)),
                  pl.BlockSpec((tb, dv), lambda b, h, p, qt, kt, sl: (b * nb + kt[p], h)),
                  vec, vec, vec, vec,
                  pl.BlockSpec((1, dv), lambda b, h, p, qt, kt, sl: (0, 0))],
        out_specs=pl.BlockSpec((tb, dv), lambda b, h, p, qt, kt, sl: (b * nb + qt[p], h)),
        scratch_shapes=[pltpu.VMEM((2, tb, LANES), F32), pltpu.VMEM((2, tb, LANES), F32),
                        pltpu.VMEM((2, tb, dv), F32)])
    return pl.pallas_call(
        functools.partial(_diff_kernel, lambda_init=lambda_init, dq=dq),
        grid_spec=grid_spec,
        out_shape=jax.ShapeDtypeStruct((m, heads * dv), BF16),
        compiler_params=_cparams(("parallel", "parallel", "arbitrary")),
        name="diff_attention",
    )(qi_tab, kj_tab, slopes, q, k, v, lq1, lk1, lq2, lk2, g)


def _own_head(rows, width, heads):
    assert heads & (heads - 1) == 0
    r = lax.broadcasted_iota(jnp.int32, (rows, width), 0)
    c = lax.broadcasted_iota(jnp.int32, (rows, width), 1)
    return (c & (heads - 1)) == (r & (heads - 1))


def _flat_pages(refs):
    return [ref[...].reshape(ref.shape[0] * ref.shape[1], ref.shape[2]).astype(BF16) for ref in refs]


def _wide_scores(q, k_refs):
    s = [_dot_nt(q, k) for k in _flat_pages(k_refs)]
    return jnp.concatenate(s, axis=1) if len(s) > 1 else s[0]


def _wide_pv(w, v_refs):
    out = None
    width = v_refs[0].shape[0] * v_refs[0].shape[1]
    for g, v in enumerate(_flat_pages(v_refs)):
        d = _dot(w[:, g * width:(g + 1) * width], v)
        out = d if out is None else out + d
    return out


def _wide_suffix_sums(x, carry, terms):
    rows, w = x.shape
    ch = min(CUM_CHUNK, w)
    n = w // ch
    u = _suffix_matrix(ch)
    parts = _bf16_terms(x, terms)
    stacked = jnp.concatenate([p[:, c * ch:(c + 1) * ch] for c in range(n) for p in parts], axis=0)
    d = _dot(stacked.astype(BF16), u)
    outs = [None] * n
    for c in reversed(range(n)):
        base = c * terms * rows
        tail = d[base:base + rows]
        for t in range(1, terms):
            tail = tail + d[base + t * rows:base + (t + 1) * rows]
        outs[c] = tail + carry
        carry = carry + jnp.sum(x[:, c * ch:(c + 1) * ch], axis=1, keepdims=True)
    return (jnp.concatenate(outs, axis=1) if n > 1 else outs[0]), carry


def _fox_dec_kernel(pt_ref, q_ref, kn_ref, vn_ref, lfn_ref, *refs, g_pages):
    lf_refs = refs[:g_pages]
    k_refs = refs[g_pages:2 * g_pages]
    v_refs = refs[2 * g_pages:3 * g_pages]
    o_ref, m_s, l_s, acc_s, carry_s = refs[3 * g_pages:]
    j = pl.program_id(1)
    heads = q_ref.shape[0]
    q = q_ref[...].astype(BF16)

    @pl.when(j == 0)
    def _():
        qn = q.astype(F32) * kn_ref[...].astype(BF16).astype(F32)
        m_s[...] = jnp.sum(qn, axis=1, keepdims=True)
        l_s[...] = jnp.ones(l_s.shape, F32)
        acc_s[...] = vn_ref[...].astype(BF16).astype(F32)
        carry_s[...] = lfn_ref[...]

    s = _wide_scores(q, k_refs)
    own = _own_head(heads, s.shape[1], heads)
    lf = jnp.concatenate([r[...] for r in lf_refs], axis=1) if g_pages > 1 else lf_refs[0][...]
    cum, carry = _wide_suffix_sums(jnp.where(own, lf, 0.0), carry_s[...], 3)
    carry_s[...] = carry
    s = jnp.where(own, s + cum, NEG)
    m_old = m_s[...]
    m_new = jnp.maximum(m_old, jnp.max(s, axis=1, keepdims=True))
    alpha = jnp.exp(m_old - m_new)
    pr = jnp.exp(s - m_new)
    l_s[...] = alpha * l_s[...] + jnp.sum(pr, axis=1, keepdims=True)
    acc_s[...] = alpha * acc_s[...] + _wide_pv(pr.astype(BF16), v_refs)
    m_s[...] = m_new

    @pl.when(j == pl.num_programs(1) - 1)
    def _():
        o_ref[...] = (acc_s[...] / l_s[...]).astype(o_ref.dtype)


def _sb_dec_kernel(pt_ref, q_ref, *refs, g_pages):
    k_refs = refs[:g_pages]
    v_refs = refs[g_pages:2 * g_pages]
    o_ref, acc_s, carry_s = refs[2 * g_pages:]
    j = pl.program_id(1)
    heads = q_ref.shape[0]
    q = q_ref[...].astype(BF16)

    @pl.when(j == 0)
    def _():
        acc_s[...] = jnp.zeros(acc_s.shape, F32)
        carry_s[...] = jnp.zeros(carry_s.shape, F32)

    z = _wide_scores(q, k_refs)
    own = _own_head(heads, z.shape[1], heads)
    log_keep = jnp.where(own, _log_sigmoid(-z), 0.0)
    tail, carry = _wide_suffix_sums(log_keep, carry_s[...], 2)
    carry_s[...] = carry
    a = jnp.where(own, jnp.exp(log_keep + z + tail), 0.0)
    acc_s[...] += _wide_pv(a.astype(BF16), v_refs)

    @pl.when(j == pl.num_programs(1) - 1)
    def _():
        o_ref[...] = acc_s[...].astype(o_ref.dtype)


def _diff_dec_kernel(pt_ref, q_ref, kn_ref, vn_ref, slope_ref, lq1_ref, lk1_ref, lq2_ref, lk2_ref,
                     g_ref, *refs, g_pages, lambda_init, n_pages):
    k_refs = refs[:g_pages]
    v_refs = refs[g_pages:2 * g_pages]
    o_ref, m_s, l_s, acc_s = refs[2 * g_pages:]
    j = pl.program_id(1)
    page, heads, _ = k_refs[0].shape
    rows = 2 * heads
    q = q_ref[...].astype(BF16)

    @pl.when(j == 0)
    def _():
        kn = kn_ref[...].astype(BF16).astype(F32)
        qn = q.astype(F32) * jnp.concatenate([kn, kn], axis=0)
        m_s[...] = jnp.sum(qn, axis=1, keepdims=True)
        l_s[...] = jnp.ones(l_s.shape, F32)
        vn = vn_ref[...].astype(BF16).astype(F32)
        acc_s[...] = jnp.concatenate([vn, vn], axis=0)

    s = _wide_scores(q, k_refs)
    own = _own_head(rows, s.shape[1], heads)
    first = (n_pages - (j + 1) * g_pages) * page
    col = lax.broadcasted_iota(jnp.int32, s.shape, 1)
    dist = (n_pages * page - first - jnp.right_shift(col, heads.bit_length() - 1)).astype(F32)
    s = jnp.where(own, s - slope_ref[...] * dist, NEG)
    m_old = m_s[...]
    m_new = jnp.maximum(m_old, jnp.max(s, axis=1, keepdims=True))
    alpha = jnp.exp(m_old - m_new)
    pr = jnp.exp(s - m_new)
    l_s[...] = alpha * l_s[...] + jnp.sum(pr, axis=1, keepdims=True)
    acc_s[...] = alpha * acc_s[...] + _wide_pv(pr.astype(BF16), v_refs)
    m_s[...] = m_new

    @pl.when(j == pl.num_programs(1) - 1)
    def _():
        lam = _lambda_value(lq1_ref, lk1_ref, lq2_ref, lk2_ref, lambda_init)
        on = acc_s[...] / l_s[...]
        o = on[:heads] - lam * on[heads:]
        o = o * lax.rsqrt(jnp.mean(o * o, axis=1, keepdims=True) + LN_EPS) * g_ref[...]
        o_ref[...] = (o * (1.0 - lambda_init)).astype(o_ref.dtype)


def _page_specs(n_pages, g_pages, block, layer):
    def make(g):
        def index(b, j, pt):
            return (layer, pt[b, n_pages - (j + 1) * g_pages + g]) + (0,) * len(block)
        return pl.BlockSpec((None, None) + block, index)
    return [make(g) for g in range(g_pages)]


def _per_seq_spec(shape):
    return pl.BlockSpec((None,) + shape, lambda b, j, pt: (b,) + (0,) * len(shape))


def _shared_spec(shape):
    return pl.BlockSpec(shape, lambda b, j, pt: (0,) * len(shape))


def _fox_decode(page_table, q, kn, vn, lfn, lf_cache, k_cache, v_cache, layer, g_pages=16):
    bsz, n_pages = page_table.shape
    _, _, page, heads, dh = k_cache.shape
    g_pages = _tile(n_pages, g_pages)
    grid_spec = pltpu.PrefetchScalarGridSpec(
        num_scalar_prefetch=1,
        grid=(bsz, n_pages // g_pages),
        in_specs=([_per_seq_spec((heads, dh))] * 3 + [_per_seq_spec((heads, 1))]
                  + _page_specs(n_pages, g_pages, (1, page * heads), layer)
                  + _page_specs(n_pages, g_pages, (page, heads, dh), layer)
                  + _page_specs(n_pages, g_pages, (page, heads, dh), layer)),
        out_specs=_per_seq_spec((heads, dh)),
        scratch_shapes=[pltpu.VMEM((heads, 1), F32), pltpu.VMEM((heads, 1), F32),
                        pltpu.VMEM((heads, dh), F32), pltpu.VMEM((heads, 1), F32)])
    return pl.pallas_call(
        functools.partial(_fox_dec_kernel, g_pages=g_pages),
        grid_spec=grid_spec,
        out_shape=jax.ShapeDtypeStruct((bsz, heads, dh), BF16),
        compiler_params=_cparams(("parallel", "arbitrary")),
        name="fox_decode",
    )(page_table, q, kn, vn, lfn, *([lf_cache] * g_pages), *([k_cache] * g_pages), *([v_cache] * g_pages))


def _sb_decode(page_table, q, k_cache, v_cache, layer, g_pages=16):
    bsz, n_pages = page_table.shape
    _, _, page, heads, dh = k_cache.shape
    g_pages = _tile(n_pages, g_pages)
    grid_spec = pltpu.PrefetchScalarGridSpec(
        num_scalar_prefetch=1,
        grid=(bsz, n_pages // g_pages),
        in_specs=([_per_seq_spec((heads, dh))]
                  + _page_specs(n_pages, g_pages, (page, heads, dh), layer)
                  + _page_specs(n_pages, g_pages, (page, heads, dh), layer)),
        out_specs=_per_seq_spec((heads, dh)),
        scratch_shapes=[pltpu.VMEM((heads, dh), F32), pltpu.VMEM((heads, 1), F32)])
    return pl.pallas_call(
        functools.partial(_sb_dec_kernel, g_pages=g_pages),
        grid_spec=grid_spec,
        out_shape=jax.ShapeDtypeStruct((bsz, heads, dh), BF16),
        compiler_params=_cparams(("parallel", "arbitrary")),
        name="sb_decode",
    )(page_table, q, *([k_cache] * g_pages), *([v_cache] * g_pages))


def _diff_decode(page_table, q, kn, vn, slopes, lq1, lk1, lq2, lk2, g, k_cache, v_cache, layer, lambda_init,
                 g_pages=8):
    bsz, n_pages = page_table.shape
    _, _, page, heads, dv = k_cache.shape
    dq = dv // 2
    rows = 2 * heads
    g_pages = _tile(n_pages, g_pages)
    grid_spec = pltpu.PrefetchScalarGridSpec(
        num_scalar_prefetch=1,
        grid=(bsz, n_pages // g_pages),
        in_specs=([_per_seq_spec((rows, dv)), _per_seq_spec((heads, dv)), _per_seq_spec((heads, dv)),
                   _shared_spec((rows, 1)),
                   _shared_spec((1, dq)), _shared_spec((1, dq)), _shared_spec((1, dq)), _shared_spec((1, dq)),
                   _shared_spec((1, dv))]
                  + _page_specs(n_pages, g_pages, (page, heads, dv), layer)
                  + _page_specs(n_pages, g_pages, (page, heads, dv), layer)),
        out_specs=_per_seq_spec((heads, dv)),
        scratch_shapes=[pltpu.VMEM((rows, 1), F32), pltpu.VMEM((rows, 1), F32),
                        pltpu.VMEM((rows, dv), F32)])
    return pl.pallas_call(
        functools.partial(_diff_dec_kernel, g_pages=g_pages, lambda_init=lambda_init, n_pages=n_pages),
        grid_spec=grid_spec,
        out_shape=jax.ShapeDtypeStruct((bsz, heads, dv), BF16),
        compiler_params=_cparams(("parallel", "arbitrary")),
        name="diff_decode",
    )(page_table, q, kn, vn, slopes, lq1, lk1, lq2, lk2, g, *([k_cache] * g_pages), *([v_cache] * g_pages))


def _even_layer(xpb, xsb, caches, page_table, w_in, b_f, w_o, dims):
    cache_k_a, cache_v_a, cache_lf, cache_k_b, cache_v_b, layer_idx = caches
    bsz, t, dbs, heads, dh = dims
    w = heads * dh
    scale = dh ** -0.5
    wb = w_in.astype(BF16)
    seg = [wb[:, i * w:(i + 1) * w] for i in range(3)] + [wb[:, 3 * w + heads + i * w:3 * w + heads + (i + 1) * w]
                                                         for i in range(3)]
    w_f = jnp.pad(wb[:, 3 * w:3 * w + heads], ((0, 0), (0, LANES - heads)))
    b_pad = jnp.pad(b_f.astype(F32), (0, LANES - heads)).reshape(1, LANES)
    wo_a, wo_b = w_o[:w].astype(BF16), w_o[w:].astype(BF16)

    def project(xb, q_dtype, q_scale):
        (q_a,) = _mm(xb, seg[0], (q_dtype,), out_scale=q_scale)
        k_a, k_ab = _mm(xb, seg[1], (F32, BF16))
        v_a, v_ab = _mm(xb, seg[2], (F32, BF16))
        (q_b,) = _mm(xb, seg[3], (q_dtype,), out_scale=q_scale)
        k_b, k_bb = _mm(xb, seg[4], (F32, BF16))
        v_b, v_bb = _mm(xb, seg[5], (F32, BF16))
        lf = _forget_gate(xb, w_f, b_pad)[:, :heads]
        return (q_a, k_a, k_ab, v_a, v_ab, q_b, k_b, k_bb, v_b, v_bb, lf)

    q_a, k_a, k_ab, v_a, v_ab, q_b, k_b, k_bb, v_b, v_bb, lf = project(xpb, BF16, scale * LOG2E)
    lf_t = jnp.swapaxes(lf.reshape(bsz, t, heads), 1, 2)
    cum = _suffix_sums(lf_t.reshape(bsz * heads, t), LOG2E).reshape(bsz, heads, 1, t)
    o_a = _fox_attention(q_a, k_ab, v_ab, cum, bsz, heads, dh)
    o_b = _sb_attention(q_b, k_bb, v_bb, bsz, heads, dh)
    new_p = (k_a.reshape(bsz, t, heads, dh), v_a.reshape(bsz, t, heads, dh), lf.reshape(bsz, t, heads),
             k_b.reshape(bsz, t, heads, dh), v_b.reshape(bsz, t, heads, dh))

    sq_a, sk_a, _, sv_a, _, sq_b, sk_b, _, sv_b, _, slf = project(xsb, F32, scale)
    per_head = lambda a: a.reshape(dbs, heads, dh)
    n_l, n_pool, page, _ = cache_lf.shape
    lf_flat = cache_lf.reshape(n_l, n_pool, 1, page * heads)
    so_a = _fox_decode(page_table, per_head(sq_a), per_head(sk_a), per_head(sv_a), slf.reshape(dbs, heads, 1),
                       lf_flat, cache_k_a, cache_v_a, layer_idx)
    so_b = _sb_decode(page_table, per_head(sq_b), cache_k_b, cache_v_b, layer_idx)
    new_s = (sk_a.reshape(dbs, 1, heads, dh), sv_a.reshape(dbs, 1, heads, dh), slf.reshape(dbs, 1, heads),
             sk_b.reshape(dbs, 1, heads, dh), sv_b.reshape(dbs, 1, heads, dh))
    return (o_a, o_b), (so_a.reshape(dbs, w), so_b.reshape(dbs, w)), (wo_a, wo_b), new_p, new_s


def _odd_layer(xpb, xsb, caches, page_table, w_in, lq1, lk1, lq2, lk2, gain, w_o, lambda_init, dims):
    cache_k, cache_v, layer_idx = caches
    bsz, t, dbs, heads, dq = dims
    dv = 2 * dq
    w = heads * dv
    scale = dq ** -0.5
    wb = w_in.astype(BF16)
    seg = [wb[:, i * w:(i + 1) * w] for i in range(3)]
    slopes = 2.0 ** (-8.0 * jnp.arange(1, heads + 1, dtype=F32) / heads)
    vecs = [a.astype(F32).reshape(1, dq) for a in (lq1, lk1, lq2, lk2)]
    g_row = gain.astype(F32).reshape(1, dv)

    def project(xb, q_dtype, q_scale):
        (q,) = _mm(xb, seg[0], (q_dtype,), out_scale=q_scale)
        k, kb = _mm(xb, seg[1], (F32, BF16))
        v, vb = _mm(xb, seg[2], (F32, BF16))
        return q, k, kb, v, vb

    q, k, kb, v, vb = project(xpb, BF16, scale * LOG2E)
    o = _diff_attention(q, kb, vb, slopes * LOG2E, *vecs, g_row, bsz, heads, dq, lambda_init)
    new_p = (k.reshape(bsz, t, heads, dv), v.reshape(bsz, t, heads, dv))

    sq, sk, _, sv, _ = project(xsb, F32, scale)
    sq4 = sq.reshape(dbs, heads, 2, dq)
    zero = jnp.zeros((dbs, heads, dq), F32)
    q_rows = jnp.concatenate([jnp.concatenate([sq4[:, :, 0], zero], axis=2),
                              jnp.concatenate([zero, sq4[:, :, 1]], axis=2)], axis=1)
    slope_rows = jnp.concatenate([slopes, slopes]).reshape(2 * heads, 1)
    so = _diff_decode(page_table, q_rows, sk.reshape(dbs, heads, dv), sv.reshape(dbs, heads, dv), slope_rows,
                      *vecs, g_row, cache_k, cache_v, layer_idx, lambda_init)
    new_s = (sk.reshape(dbs, 1, heads, dv), sv.reshape(dbs, 1, heads, dv))
    return (o,), (so.reshape(dbs, w),), (w_o.astype(BF16),), new_p, new_s


def kernel(x_prompt, x_sample, cache_fox_k, cache_fox_v, cache_fox_logf, cache_sb_k, cache_sb_v, cache_diff_k, cache_diff_v, page_table, w_in_even, b_forget, w_out_even, w_in_odd, lambda_q1, lambda_k1, lambda_q2, lambda_k2, subln_gain, w_out_odd, ln_mix_gain, ln_mix_bias, ln_ffn_gain, ln_ffn_bias, w_ffn_up, w_ffn_down):
    bsz, t, d = x_prompt.shape
    dbs, ts, _ = x_sample.shape
    assert ts == 1, "the sample group carries one new token per sequence"
    depth = ln_mix_gain.shape[0]
    alpha = (2 * depth) ** 0.25
    h_fox, dh = cache_fox_k.shape[3], cache_fox_k.shape[4]
    h_diff, dq = cache_diff_k.shape[3], cache_diff_k.shape[4] // 2
    assert cache_sb_k.shape[3:] == (h_fox, dh), "both head groups of an even layer share one geometry"

    xp = x_prompt.reshape(bsz * t, d)
    xs = x_sample.reshape(dbs, d)
    xpb, xsb = xp.astype(BF16), xs.astype(BF16)
    even_p, even_s, odd_p, odd_s = [], [], [], []
    row = lambda a: a.astype(F32).reshape(1, d)
    for layer in range(depth):
        i = layer // 2
        if layer % 2 == 0:
            caches = (cache_fox_k, cache_fox_v, cache_fox_logf, cache_sb_k, cache_sb_v, i)
            o_p, o_s, w_o, new_p, new_s = _even_layer(xpb, xsb, caches, page_table, w_in_even[i],
                                                      b_forget[i], w_out_even[i], (bsz, t, dbs, h_fox, dh))
            even_p.append(new_p)
            even_s.append(new_s)
        else:
            lambda_init = 0.8 - 0.6 * math.exp(-0.3 * layer)
            caches = (cache_diff_k, cache_diff_v, i)
            o_p, o_s, w_o, new_p, new_s = _odd_layer(xpb, xsb, caches, page_table, w_in_odd[i], lambda_q1[i],
                                                     lambda_k1[i], lambda_q2[i], lambda_k2[i], subln_gain[i],
                                                     w_out_odd[i], lambda_init, (bsz, t, dbs, h_diff, dq))
            odd_p.append(new_p)
            odd_s.append(new_s)
        g1, b1 = row(ln_mix_gain[layer]), row(ln_mix_bias[layer])
        g2, b2 = row(ln_ffn_gain[layer]), row(ln_ffn_bias[layer])
        w_up, w_down = w_ffn_up[layer].astype(BF16), w_ffn_down[layer].astype(BF16)
        xp, xpb = _proj_residual_ln(list(o_p), list(w_o), xp, g1, b1, alpha)
        xs, xsb = _proj_residual_ln(list(o_s), list(w_o), xs, g1, b1, alpha)
        xp, xpb = _ffn_residual_ln(xpb, xp, w_up, w_down, g2, b2, alpha)
        xs, xsb = _ffn_residual_ln(xsb, xs, w_up, w_down, g2, b2, alpha)

    def stk(rows, j):
        return jnp.stack([r[j] for r in rows])

    return (xp.reshape(bsz, t, d), xs.reshape(dbs, 1, d),
            stk(even_p, 0), stk(even_p, 1), stk(even_p, 2), stk(even_p, 3), stk(even_p, 4), stk(odd_p, 0), stk(odd_p, 1),
            stk(even_s, 0), stk(even_s, 1), stk(even_s, 2), stk(even_s, 3), stk(even_s, 4), stk(odd_s, 0), stk(odd_s, 1))
```

```python
import functools
import math

import jax
import jax.numpy as jnp
from jax import lax
from jax.experimental import pallas as pl
from jax.experimental.pallas import tpu as pltpu

F32 = jnp.float32
BF16 = jnp.bfloat16
LN_EPS = 1e-5
NEG = -1e30
LOG2E = 1.4426950408889634
LANES = 128
CUM_CHUNK = 256
DIAG_STRIP = 256
VMEM_LIMIT = 56 * 1024 * 1024


def _cparams(sem):
    return pltpu.CompilerParams(dimension_semantics=sem, vmem_limit_bytes=VMEM_LIMIT)


def _tile(n, pref):
    t = min(n, pref)
    while n % t:
        t //= 2
    return t


def _dot(a, b):
    return jnp.dot(a, b, preferred_element_type=F32)


def _dot_nt(a, b):
    return lax.dot_general(a, b, (((1,), (1,)), ((), ())), preferred_element_type=F32)


def _bf16_terms(x, terms):
    out = []
    r = x
    for t in range(terms):
        h = r.astype(BF16).astype(F32)
        out.append(h)
        if t + 1 < terms:
            r = r - h
    return out


def _split_dot(x, u, terms):
    out = None
    for h in _bf16_terms(x, terms):
        d = _dot(h.astype(BF16), u)
        out = d if out is None else out + d
    return out


def _suffix_matrix(n):
    r = lax.broadcasted_iota(jnp.int32, (n, n), 0)
    c = lax.broadcasted_iota(jnp.int32, (n, n), 1)
    return jnp.where(r > c, 1.0, 0.0).astype(BF16)


def _log_sigmoid(x):
    return jnp.minimum(x, 0.0) - jnp.log1p(jnp.exp(-jnp.abs(x)))


def _layer_norm(x, g, b):
    mu = jnp.mean(x, axis=-1, keepdims=True)
    xc = x - mu
    var = jnp.mean(xc * xc, axis=-1, keepdims=True)
    return xc * lax.rsqrt(var + LN_EPS) * g + b


def _mm_kernel(x_ref, w_ref, *o_refs, out_scale):
    acc = _dot(x_ref[...], w_ref[...])
    if out_scale != 1.0:
        acc = acc * out_scale
    for o_ref in o_refs:
        o_ref[...] = acc.astype(o_ref.dtype)


def _mm(x, w, out_dtypes, out_scale=1.0, col0=0, n=None, tm=1024, tn=1024):
    m, k = x.shape
    n = w.shape[1] - col0 if n is None else n
    tm, tn = _tile(m, tm), _tile(n, tn)
    assert col0 % tn == 0, "the column window must start on a column-block boundary"
    j0 = col0 // tn
    outs = pl.pallas_call(
        functools.partial(_mm_kernel, out_scale=out_scale),
        grid=(n // tn, m // tm),
        in_specs=[pl.BlockSpec((tm, k), lambda j, i: (i, 0)),
                  pl.BlockSpec((k, tn), lambda j, i: (0, j0 + j))],
        out_specs=[pl.BlockSpec((tm, tn), lambda j, i: (i, j)) for _ in out_dtypes],
        out_shape=[jax.ShapeDtypeStruct((m, n), d) for d in out_dtypes],
        compiler_params=_cparams(("parallel", "parallel")),
        name="proj",
    )(x, w)
    return outs


def _gate_kernel(x_ref, w_ref, b_ref, o_ref):
    o_ref[...] = _log_sigmoid(_dot(x_ref[...], w_ref[...]) + b_ref[...])


def _forget_gate(x, w, b, tm=512):
    m, k = x.shape
    n = w.shape[1]
    tm = _tile(m, tm)
    return pl.pallas_call(
        _gate_kernel,
        grid=(m // tm,),
        in_specs=[pl.BlockSpec((tm, k), lambda i: (i, 0)),
                  pl.BlockSpec((k, n), lambda i: (0, 0)),
                  pl.BlockSpec((1, n), lambda i: (0, 0))],
        out_specs=pl.BlockSpec((tm, n), lambda i: (i, 0)),
        out_shape=jax.ShapeDtypeStruct((m, n), F32),
        compiler_params=_cparams(("parallel",)),
        name="forget_gate",
    )(x, w, b)


def _revcum_kernel(x_ref, o_ref, *, out_scale):
    rows, t = x_ref.shape
    ch = min(CUM_CHUNK, t)
    u = _suffix_matrix(ch)
    carry = jnp.zeros((rows, 1), F32)
    for c in reversed(range(t // ch)):
        xc = x_ref[:, c * ch:(c + 1) * ch]
        o_ref[:, c * ch:(c + 1) * ch] = (_split_dot(xc, u, 3) + carry) * out_scale
        carry = carry + jnp.sum(xc, axis=1, keepdims=True)


def _suffix_sums(x, out_scale):
    return pl.pallas_call(
        functools.partial(_revcum_kernel, out_scale=out_scale),
        out_shape=jax.ShapeDtypeStruct(x.shape, F32),
        name="suffix_sums",
    )(x)


def _proj_ln_kernel(*refs, n_pairs, alpha):
    a_refs = refs[:n_pairs]
    w_refs = refs[n_pairs:2 * n_pairs]
    x_ref, g_ref, b_ref, y_ref, yb_ref = refs[2 * n_pairs:]
    acc = alpha * x_ref[...]
    for a_ref, w_ref in zip(a_refs, w_refs):
        acc = acc + _dot(a_ref[...], w_ref[...])
    y = _layer_norm(acc, g_ref[...], b_ref[...])
    y_ref[...] = y
    yb_ref[...] = y.astype(BF16)


def _proj_residual_ln(a_list, w_list, x, g, b, alpha, tm=512):
    m, d = x.shape
    tm = _tile(m, tm)
    n_pairs = len(a_list)
    in_specs = ([pl.BlockSpec((tm, a.shape[1]), lambda i: (i, 0)) for a in a_list]
                + [pl.BlockSpec(w.shape, lambda i: (0, 0)) for w in w_list]
                + [pl.BlockSpec((tm, d), lambda i: (i, 0)),
                   pl.BlockSpec((1, d), lambda i: (0, 0)),
                   pl.BlockSpec((1, d), lambda i: (0, 0))])
    return pl.pallas_call(
        functools.partial(_proj_ln_kernel, n_pairs=n_pairs, alpha=alpha),
        grid=(m // tm,),
        in_specs=in_specs,
        out_specs=[pl.BlockSpec((tm, d), lambda i: (i, 0)), pl.BlockSpec((tm, d), lambda i: (i, 0))],
        out_shape=[jax.ShapeDtypeStruct((m, d), F32), jax.ShapeDtypeStruct((m, d), BF16)],
        compiler_params=_cparams(("parallel",)),
        name="out_proj_ln",
    )(*a_list, *w_list, x, g, b)


def _ffn_kernel(xb_ref, x_ref, wu_ref, wd_ref, g_ref, b_ref, y_ref, yb_ref, acc_ref, *, alpha):
    f = pl.program_id(1)

    @pl.when(f == 0)
    def _():
        acc_ref[...] = alpha * x_ref[...]

    h = jnp.maximum(_dot(xb_ref[...], wu_ref[...]), 0.0)
    acc_ref[...] += _dot((h * h).astype(BF16), wd_ref[...])

    @pl.when(f == pl.num_programs(1) - 1)
    def _():
        y = _layer_norm(acc_ref[...], g_ref[...], b_ref[...])
        y_ref[...] = y
        yb_ref[...] = y.astype(BF16)


def _ffn_residual_ln(xb, x, w_up, w_down, layer, g, b, alpha, tm=512, tf=1024):
    m, d = x.shape
    dff = w_up.shape[2]
    tm, tf = _tile(m, tm), _tile(dff, tf)
    return pl.pallas_call(
        functools.partial(_ffn_kernel, alpha=alpha),
        grid=(m // tm, dff // tf),
        in_specs=[pl.BlockSpec((tm, d), lambda i, f: (i, 0)),
                  pl.BlockSpec((tm, d), lambda i, f: (i, 0)),
                  pl.BlockSpec((None, d, tf), lambda i, f: (layer, 0, f)),
                  pl.BlockSpec((None, tf, d), lambda i, f: (layer, f, 0)),
                  pl.BlockSpec((1, d), lambda i, f: (0, 0)),
                  pl.BlockSpec((1, d), lambda i, f: (0, 0))],
        out_specs=[pl.BlockSpec((tm, d), lambda i, f: (i, 0)), pl.BlockSpec((tm, d), lambda i, f: (i, 0))],
        out_shape=[jax.ShapeDtypeStruct((m, d), F32), jax.ShapeDtypeStruct((m, d), BF16)],
        scratch_shapes=[pltpu.VMEM((tm, d), F32)],
        compiler_params=_cparams(("parallel", "arbitrary")),
        name="ffn_ln",
    )(xb, x, w_up, w_down, g, b)


def _pair_tables(n_blocks, descending):
    qi, kj = [], []
    for q in range(n_blocks):
        ks = range(q, -1, -1) if descending else range(q + 1)
        for k in ks:
            qi.append(q)
            kj.append(k)
    return jnp.asarray(qi, jnp.int32), jnp.asarray(kj, jnp.int32)


def _tri_iota(t):
    r = lax.broadcasted_iota(jnp.int32, (t, t), 0)
    c = lax.broadcasted_iota(jnp.int32, (t, t), 1)
    return r, c


def _causal(rows, width, row0=0, col0=0, strict=False):
    r = jnp.tile(lax.broadcasted_iota(jnp.int32, (rows, LANES), 0), (1, width // LANES)) + row0
    c = lax.broadcasted_iota(jnp.int32, (1, width), 1) + col0
    return (c < r) if strict else (c <= r)


def _row_strips(t, diagonal, rows=DIAG_STRIP):
    if not diagonal:
        return [(0, t, t)]
    rows = min(rows, t)
    return [(r0, r0 + rows, r0 + rows) for r0 in range(0, t, rows)]


def _lanes_to_rows(row, t):
    r, c = _tri_iota(t)
    col = jnp.sum(jnp.where(r == c, row, 0.0), axis=1, keepdims=True)
    return jnp.broadcast_to(col, (t, LANES))


def _softmax_step(a, row_shift, m_prev, l_prev, acc_prev, v):
    t, tk = a.shape
    m_next = jnp.maximum(m_prev, jnp.max(a, axis=1, keepdims=True) - row_shift)
    pr = jnp.exp2(a - jnp.tile(m_next + row_shift, (1, tk // LANES)))
    alpha = jnp.exp2(m_prev - m_next)
    l_next = alpha * l_prev + jnp.sum(pr, axis=1, keepdims=True)
    acc_next = jnp.tile(alpha, (1, v.shape[1] // LANES)) * acc_prev + _dot(pr.astype(BF16), v)
    return m_next, l_next, acc_next


def _fox_kernel(qi_tab, kj_tab, q_ref, k_ref, v_ref, ck_ref, cqrow_ref, o_ref, m_s, l_s, acc_s, cq_s, *, dh):
    p = pl.program_id(2)
    qi, kj = qi_tab[p], kj_tab[p]
    t = q_ref.shape[0]
    hb = m_s.shape[0]

    @pl.when(kj == 0)
    def _():
        m_s[...] = jnp.full(m_s.shape, -jnp.inf, F32)
        l_s[...] = jnp.zeros(l_s.shape, F32)
        acc_s[...] = jnp.zeros(acc_s.shape, F32)
        for i in range(hb):
            cq_s[i] = _lanes_to_rows(cqrow_ref[i], t)

    def step(diagonal):
        for i in range(hb):
            cols = slice(i * dh, (i + 1) * dh)
            for r0, r1, kc in _row_strips(t, diagonal):
                a = _dot_nt(q_ref[r0:r1, cols], k_ref[:kc, cols]) + ck_ref[i, :, :kc]
                if diagonal:
                    a = jnp.where(_causal(r1 - r0, kc, row0=r0), a, NEG)
                m, l, acc = _softmax_step(a, cq_s[i, r0:r1], m_s[i, r0:r1], l_s[i, r0:r1], acc_s[i, r0:r1],
                                          v_ref[:kc, cols])
                m_s[i, r0:r1] = m
                l_s[i, r0:r1] = l
                acc_s[i, r0:r1] = acc

    @pl.when(kj < qi)
    def _():
        step(False)

    @pl.when(kj == qi)
    def _():
        step(True)
        for i in range(hb):
            o_ref[:, i * dh:(i + 1) * dh] = (acc_s[i] / l_s[i]).astype(o_ref.dtype)


def _fox_attention(q, k, v, cum, batch, heads, dh, tb=512, hb=4):
    assert dh == LANES
    m = q.shape[0]
    t = m // batch
    tb = _tile(t, tb)
    hb = _tile(heads, hb)
    nb = t // tb
    qi_tab, kj_tab = _pair_tables(nb, descending=False)
    grid_spec = pltpu.PrefetchScalarGridSpec(
        num_scalar_prefetch=2,
        grid=(batch, heads // hb, int(qi_tab.shape[0])),
        in_specs=[pl.BlockSpec((tb, hb * dh), lambda b, h, p, qt, kt: (b * nb + qt[p], h)),
                  pl.BlockSpec((tb, hb * dh), lambda b, h, p, qt, kt: (b * nb + kt[p], h)),
                  pl.BlockSpec((tb, hb * dh), lambda b, h, p, qt, kt: (b * nb + kt[p], h)),
                  pl.BlockSpec((None, hb, 1, tb), lambda b, h, p, qt, kt: (b, h, 0, kt[p])),
                  pl.BlockSpec((None, hb, 1, tb), lambda b, h, p, qt, kt: (b, h, 0, qt[p]))],
        out_specs=pl.BlockSpec((tb, hb * dh), lambda b, h, p, qt, kt: (b * nb + qt[p], h)),
        scratch_shapes=[pltpu.VMEM((hb, tb, LANES), F32), pltpu.VMEM((hb, tb, LANES), F32),
                        pltpu.VMEM((hb, tb, dh), F32), pltpu.VMEM((hb, tb, LANES), F32)])
    return pl.pallas_call(
        functools.partial(_fox_kernel, dh=dh),
        grid_spec=grid_spec,
        out_shape=jax.ShapeDtypeStruct((m, heads * dh), BF16),
        compiler_params=_cparams(("parallel", "parallel", "arbitrary")),
        name="fox_attention",
    )(qi_tab, kj_tab, q, k, v, cum, cum)


def _sb_kernel(qi_tab, kj_tab, q_ref, k_ref, v_ref, o_ref, carry_s, acc_s, *, dh):
    p = pl.program_id(2)
    qi, kj = qi_tab[p], kj_tab[p]
    t = q_ref.shape[0]
    ch = min(CUM_CHUNK, t)
    hb = acc_s.shape[0]

    @pl.when(kj == qi)
    def _():
        carry_s[...] = jnp.zeros(carry_s.shape, F32)
        acc_s[...] = jnp.zeros(acc_s.shape, F32)

    def step(diagonal):
        u = _suffix_matrix(ch)
        for i in range(hb):
            cols = slice(i * dh, (i + 1) * dh)
            for r0, r1, kc in _row_strips(t, diagonal, ch):
                drop = carry_s[i, r0:r1]
                acc = acc_s[i, r0:r1]
                for c0 in reversed(range(0, kc, ch)):
                    z = _dot_nt(q_ref[r0:r1, cols], k_ref[c0:c0 + ch, cols])
                    sp = jnp.maximum(z, 0.0) + jnp.log2(1.0 + jnp.exp2(-jnp.abs(z)))
                    on_diagonal = diagonal and c0 + ch > r0
                    if on_diagonal:
                        strict = _causal(r1 - r0, ch, row0=r0, col0=c0, strict=True)
                        sp = jnp.where(strict, sp, 0.0)
                    hi, lo = (x.astype(BF16) for x in _bf16_terms(sp, 2))
                    a = jnp.exp2(z - sp - (_dot(hi, u) + _dot(lo, u)) - jnp.tile(drop, (1, ch // LANES)))
                    if on_diagonal:
                        a = jnp.where(strict, a, 0.0)
                    acc = acc + _dot(a.astype(BF16), v_ref[c0:c0 + ch, cols])
                    drop = drop + jnp.sum(sp, axis=1, keepdims=True)
                carry_s[i, r0:r1] = drop
                acc_s[i, r0:r1] = acc

    @pl.when(kj == qi)
    def _():
        step(True)

    @pl.when(kj < qi)
    def _():
        step(False)

    @pl.when(kj == 0)
    def _():
        for i in range(hb):
            o_ref[:, i * dh:(i + 1) * dh] = acc_s[i].astype(o_ref.dtype)


def _sb_attention(q, k, v, batch, heads, dh, tb=512, hb=4):
    m = q.shape[0]
    t = m // batch
    tb = _tile(t, tb)
    hb = _tile(heads, hb)
    nb = t // tb
    qi_tab, kj_tab = _pair_tables(nb, descending=True)
    grid_spec = pltpu.PrefetchScalarGridSpec(
        num_scalar_prefetch=2,
        grid=(batch, heads // hb, int(qi_tab.shape[0])),
        in_specs=[pl.BlockSpec((tb, hb * dh), lambda b, h, p, qt, kt: (b * nb + qt[p], h)),
                  pl.BlockSpec((tb, hb * dh), lambda b, h, p, qt, kt: (b * nb + kt[p], h)),
                  pl.BlockSpec((tb, hb * dh), lambda b, h, p, qt, kt: (b * nb + kt[p], h))],
        out_specs=pl.BlockSpec((tb, hb * dh), lambda b, h, p, qt, kt: (b * nb + qt[p], h)),
        scratch_shapes=[pltpu.VMEM((hb, tb, LANES), F32), pltpu.VMEM((hb, tb, dh), F32)])
    return pl.pallas_call(
        functools.partial(_sb_kernel, dh=dh),
        grid_spec=grid_spec,
        out_shape=jax.ShapeDtypeStruct((m, heads * dh), BF16),
        compiler_params=_cparams(("parallel", "parallel", "arbitrary")),
        name="sb_attention",
    )(qi_tab, kj_tab, q, k, v)


def _lambda_value(lq1_ref, lk1_ref, lq2_ref, lk2_ref, lambda_init):
    a = jnp.sum(lq1_ref[...] * lk1_ref[...], axis=1, keepdims=True)
    b = jnp.sum(lq2_ref[...] * lk2_ref[...], axis=1, keepdims=True)
    return jnp.exp(a) - jnp.exp(b) + lambda_init


def _diff_kernel(qi_tab, kj_tab, slope_ref, q_ref, k_ref, v_ref, lq1_ref, lk1_ref, lq2_ref, lk2_ref,
                 g_ref, o_ref, m_s, l_s, acc_s, *, lambda_init, dq):
    hg = pl.program_id(1)
    p = pl.program_id(2)
    qi, kj = qi_tab[p], kj_tab[p]
    t = q_ref.shape[0]
    dv = 2 * dq
    hb = m_s.shape[0] // 2

    @pl.when(kj == 0)
    def _():
        m_s[...] = jnp.full(m_s.shape, -jnp.inf, F32)
        l_s[...] = jnp.zeros(l_s.shape, F32)
        acc_s[...] = jnp.zeros(acc_s.shape, F32)

    def step(diagonal):
        for j in range(hb):
            slope = slope_ref[hg * hb + j]
            col_bias = slope * lax.broadcasted_iota(jnp.int32, (1, t), 1).astype(F32)
            row_pos = (qi - kj) * t + lax.broadcasted_iota(jnp.int32, (t, LANES), 0)
            row_shift = slope * row_pos.astype(F32)
            for i in range(2):
                cols = slice(j * dv + i * dq, j * dv + (i + 1) * dq)
                s = 2 * j + i
                for r0, r1, kc in _row_strips(t, diagonal):
                    a = _dot_nt(q_ref[r0:r1, cols], k_ref[:kc, cols]) + col_bias[:, :kc]
                    if diagonal:
                        a = jnp.where(_causal(r1 - r0, kc, row0=r0), a, NEG)
                    m, l, acc = _softmax_step(a, row_shift[r0:r1], m_s[s, r0:r1], l_s[s, r0:r1], acc_s[s, r0:r1],
                                              v_ref[:kc, j * dv:(j + 1) * dv])
                    m_s[s, r0:r1] = m
                    l_s[s, r0:r1] = l
                    acc_s[s, r0:r1] = acc

    @pl.when(kj < qi)
    def _():
        step(False)

    @pl.when(kj == qi)
    def _():
        step(True)
        lam = _lambda_value(lq1_ref, lk1_ref, lq2_ref, lk2_ref, lambda_init)
        reps = dv // LANES
        for j in range(hb):
            o = (acc_s[2 * j] / jnp.tile(l_s[2 * j], (1, reps))
                 - lam * (acc_s[2 * j + 1] / jnp.tile(l_s[2 * j + 1], (1, reps))))
            o = o * lax.rsqrt(jnp.mean(o * o, axis=1, keepdims=True) + LN_EPS) * g_ref[...]
            o_ref[:, j * dv:(j + 1) * dv] = (o * (1.0 - lambda_init)).astype(o_ref.dtype)


def _diff_attention(q, k, v, slopes, lq1, lk1, lq2, lk2, g, batch, heads, dq, lambda_init, tb=512, hb=4):
    m = q.shape[0]
    t = m // batch
    tb = _tile(t, tb)
    hb = _tile(heads, hb)
    nb = t // tb
    dv = 2 * dq
    qi_tab, kj_tab = _pair_tables(nb, descending=False)
    vec = pl.BlockSpec((1, dq), lambda b, h, p, qt, kt, sl: (0, 0))
    grid_spec = pltpu.PrefetchScalarGridSpec(
        num_scalar_prefetch=3,
        grid=(batch, heads // hb, int(qi_tab.shape[0])),
        in_specs=[pl.BlockSpec((tb, hb * dv), lambda b, h, p, qt, kt, sl: (b * nb + qt[p], h)),
                  pl.BlockSpec((tb, hb * dv), lambda b, h, p, qt, kt, sl: (b * nb + kt[p], h)),
                  pl.BlockSpec((tb, hb * dv), lambda b, h, p, qt, kt, sl: (b * nb + kt[p], h)),
                  vec, vec, vec, vec,
                  pl.BlockSpec((1, dv), lambda b, h, p, qt, kt, sl: (0, 0))],
        out_specs=pl.BlockSpec((tb, hb * dv), lambda b, h, p, qt, kt, sl: (b * nb + qt[p], h)),
        scratch_shapes=[pltpu.VMEM((2 * hb, tb, LANES), F32), pltpu.VMEM((2 * hb, tb, LANES), F32),
                        pltpu.VMEM((2 * hb, tb, dv), F32)])
    return pl.pallas_call(
        functools.partial(_diff_kernel, lambda_init=lambda_init, dq=dq),
        grid_spec=grid_spec,
        out_shape=jax.ShapeDtypeStruct((m, heads * dv), BF16),
        compiler_params=_cparams(("parallel", "parallel", "arbitrary")),
        name="diff_attention",
    )(qi_tab, kj_tab, slopes, q, k, v, lq1, lk1, lq2, lk2, g)


def _own_head(rows, width, heads):
    assert heads & (heads - 1) == 0
    r = lax.broadcasted_iota(jnp.int32, (rows, width), 0)
    c = lax.broadcasted_iota(jnp.int32, (rows, width), 1)
    return (c & (heads - 1)) == (r & (heads - 1))


def _flat_pages(refs):
    return [ref[...].reshape(ref.shape[0] * ref.shape[1], ref.shape[2]).astype(BF16) for ref in refs]


def _wide_scores(q, k_refs):
    s = [_dot_nt(q, k) for k in _flat_pages(k_refs)]
    return jnp.concatenate(s, axis=1) if len(s) > 1 else s[0]


def _wide_pv(w, v_refs):
    out = None
    width = v_refs[0].shape[0] * v_refs[0].shape[1]
    for g, v in enumerate(_flat_pages(v_refs)):
        d = _dot(w[:, g * width:(g + 1) * width], v)
        out = d if out is None else out + d
    return out


def _wide_suffix_sums(x, carry, terms):
    rows, w = x.shape
    ch = min(CUM_CHUNK, w)
    n = w // ch
    u = _suffix_matrix(ch)
    parts = _bf16_terms(x, terms)
    stacked = jnp.concatenate([p[:, c * ch:(c + 1) * ch] for c in range(n) for p in parts], axis=0)
    d = _dot(stacked.astype(BF16), u)
    outs = [None] * n
    for c in reversed(range(n)):
        base = c * terms * rows
        tail = d[base:base + rows]
        for t in range(1, terms):
            tail = tail + d[base + t * rows:base + (t + 1) * rows]
        outs[c] = tail + carry
        carry = carry + jnp.sum(x[:, c * ch:(c + 1) * ch], axis=1, keepdims=True)
    return (jnp.concatenate(outs, axis=1) if n > 1 else outs[0]), carry


def _fox_dec_kernel(pt_ref, q_ref, kn_ref, vn_ref, lfn_ref, *refs, g_pages):
    lf_refs = refs[:g_pages]
    k_refs = refs[g_pages:2 * g_pages]
    v_refs = refs[2 * g_pages:3 * g_pages]
    o_ref, m_s, l_s, acc_s, carry_s = refs[3 * g_pages:]
    j = pl.program_id(1)
    heads = q_ref.shape[0]
    q = q_ref[...].astype(BF16)

    @pl.when(j == 0)
    def _():
        qn = q.astype(F32) * kn_ref[...].astype(BF16).astype(F32)
        m_s[...] = jnp.sum(qn, axis=1, keepdims=True)
        l_s[...] = jnp.ones(l_s.shape, F32)
        acc_s[...] = vn_ref[...].astype(BF16).astype(F32)
        carry_s[...] = lfn_ref[...]

    s = _wide_scores(q, k_refs)
    own = _own_head(heads, s.shape[1], heads)
    lf = jnp.concatenate([r[...] for r in lf_refs], axis=1) if g_pages > 1 else lf_refs[0][...]
    cum, carry = _wide_suffix_sums(jnp.where(own, lf, 0.0), carry_s[...], 3)
    carry_s[...] = carry
    s = jnp.where(own, s + cum, NEG)
    m_old = m_s[...]
    m_new = jnp.maximum(m_old, jnp.max(s, axis=1, keepdims=True))
    alpha = jnp.exp(m_old - m_new)
    pr = jnp.exp(s - m_new)
    l_s[...] = alpha * l_s[...] + jnp.sum(pr, axis=1, keepdims=True)
    acc_s[...] = alpha * acc_s[...] + _wide_pv(pr.astype(BF16), v_refs)
    m_s[...] = m_new

    @pl.when(j == pl.num_programs(1) - 1)
    def _():
        o_ref[...] = (acc_s[...] / l_s[...]).astype(o_ref.dtype)


def _sb_dec_kernel(pt_ref, q_ref, *refs, g_pages):
    k_refs = refs[:g_pages]
    v_refs = refs[g_pages:2 * g_pages]
    o_ref, acc_s, carry_s = refs[2 * g_pages:]
    j = pl.program_id(1)
    heads = q_ref.shape[0]
    q = q_ref[...].astype(BF16)

    @pl.when(j == 0)
    def _():
        acc_s[...] = jnp.zeros(acc_s.shape, F32)
        carry_s[...] = jnp.zeros(carry_s.shape, F32)

    z = _wide_scores(q, k_refs)
    own = _own_head(heads, z.shape[1], heads)
    log_keep = jnp.where(own, _log_sigmoid(-z), 0.0)
    tail, carry = _wide_suffix_sums(log_keep, carry_s[...], 2)
    carry_s[...] = carry
    a = jnp.where(own, jnp.exp(log_keep + z + tail), 0.0)
    acc_s[...] += _wide_pv(a.astype(BF16), v_refs)

    @pl.when(j == pl.num_programs(1) - 1)
    def _():
        o_ref[...] = acc_s[...].astype(o_ref.dtype)


def _diff_dec_kernel(pt_ref, q_ref, kn_ref, vn_ref, slope_ref, lq1_ref, lk1_ref, lq2_ref, lk2_ref,
                     g_ref, *refs, g_pages, lambda_init, n_pages):
    k_refs = refs[:g_pages]
    v_refs = refs[g_pages:2 * g_pages]
    o_ref, m_s, l_s, acc_s = refs[2 * g_pages:]
    j = pl.program_id(1)
    page, heads, _ = k_refs[0].shape
    rows = 2 * heads
    q = q_ref[...].astype(BF16)

    @pl.when(j == 0)
    def _():
        kn = kn_ref[...].astype(BF16).astype(F32)
        qn = q.astype(F32) * jnp.concatenate([kn, kn], axis=0)
        m_s[...] = jnp.sum(qn, axis=1, keepdims=True)
        l_s[...] = jnp.ones(l_s.shape, F32)
        vn = vn_ref[...].astype(BF16).astype(F32)
        acc_s[...] = jnp.concatenate([vn, vn], axis=0)

    s = _wide_scores(q, k_refs)
    own = _own_head(rows, s.shape[1], heads)
    first = (n_pages - (j + 1) * g_pages) * page
    col = lax.broadcasted_iota(jnp.int32, s.shape, 1)
    dist = (n_pages * page - first - jnp.right_shift(col, heads.bit_length() - 1)).astype(F32)
    s = jnp.where(own, s - slope_ref[...] * dist, NEG)
    m_old = m_s[...]
    m_new = jnp.maximum(m_old, jnp.max(s, axis=1, keepdims=True))
    alpha = jnp.exp(m_old - m_new)
    pr = jnp.exp(s - m_new)
    l_s[...] = alpha * l_s[...] + jnp.sum(pr, axis=1, keepdims=True)
    acc_s[...] = alpha * acc_s[...] + _wide_pv(pr.astype(BF16), v_refs)
    m_s[...] = m_new

    @pl.when(j == pl.num_programs(1) - 1)
    def _():
        lam = _lambda_value(lq1_ref, lk1_ref, lq2_ref, lk2_ref, lambda_init)
        on = acc_s[...] / l_s[...]
        o = on[:heads] - lam * on[heads:]
        o = o * lax.rsqrt(jnp.mean(o * o, axis=1, keepdims=True) + LN_EPS) * g_ref[...]
        o_ref[...] = (o * (1.0 - lambda_init)).astype(o_ref.dtype)


def _page_specs(n_pages, g_pages, block, layer):
    def make(g):
        def index(b, j, pt):
            return (layer, pt[b, n_pages - (j + 1) * g_pages + g]) + (0,) * len(block)
        return pl.BlockSpec((None, None) + block, index)
    return [make(g) for g in range(g_pages)]


def _per_seq_spec(shape):
    return pl.BlockSpec((None,) + shape, lambda b, j, pt: (b,) + (0,) * len(shape))


def _shared_spec(shape):
    return pl.BlockSpec(shape, lambda b, j, pt: (0,) * len(shape))


def _fox_decode(page_table, q, kn, vn, lfn, lf_cache, k_cache, v_cache, layer, g_pages=16):
    bsz, n_pages = page_table.shape
    _, _, page, heads, dh = k_cache.shape
    g_pages = _tile(n_pages, g_pages)
    grid_spec = pltpu.PrefetchScalarGridSpec(
        num_scalar_prefetch=1,
        grid=(bsz, n_pages // g_pages),
        in_specs=([_per_seq_spec((heads, dh))] * 3 + [_per_seq_spec((heads, 1))]
                  + _page_specs(n_pages, g_pages, (1, page * heads), layer)
                  + _page_specs(n_pages, g_pages, (page, heads, dh), layer)
                  + _page_specs(n_pages, g_pages, (page, heads, dh), layer)),
        out_specs=_per_seq_spec((heads, dh)),
        scratch_shapes=[pltpu.VMEM((heads, 1), F32), pltpu.VMEM((heads, 1), F32),
                        pltpu.VMEM((heads, dh), F32), pltpu.VMEM((heads, 1), F32)])
    return pl.pallas_call(
        functools.partial(_fox_dec_kernel, g_pages=g_pages),
        grid_spec=grid_spec,
        out_shape=jax.ShapeDtypeStruct((bsz, heads, dh), BF16),
        compiler_params=_cparams(("parallel", "arbitrary")),
        name="fox_decode",
    )(page_table, q, kn, vn, lfn, *([lf_cache] * g_pages), *([k_cache] * g_pages), *([v_cache] * g_pages))


def _sb_decode(page_table, q, k_cache, v_cache, layer, g_pages=16):
    bsz, n_pages = page_table.shape
    _, _, page, heads, dh = k_cache.shape
    g_pages = _tile(n_pages, g_pages)
    grid_spec = pltpu.PrefetchScalarGridSpec(
        num_scalar_prefetch=1,
        grid=(bsz, n_pages // g_pages),
        in_specs=([_per_seq_spec((heads, dh))]
                  + _page_specs(n_pages, g_pages, (page, heads, dh), layer)
                  + _page_specs(n_pages, g_pages, (page, heads, dh), layer)),
        out_specs=_per_seq_spec((heads, dh)),
        scratch_shapes=[pltpu.VMEM((heads, dh), F32), pltpu.VMEM((heads, 1), F32)])
    return pl.pallas_call(
        functools.partial(_sb_dec_kernel, g_pages=g_pages),
        grid_spec=grid_spec,
        out_shape=jax.ShapeDtypeStruct((bsz, heads, dh), BF16),
        compiler_params=_cparams(("parallel", "arbitrary")),
        name="sb_decode",
    )(page_table, q, *([k_cache] * g_pages), *([v_cache] * g_pages))


def _diff_decode(page_table, q, kn, vn, slopes, lq1, lk1, lq2, lk2, g, k_cache, v_cache, layer, lambda_init,
                 g_pages=8):
    bsz, n_pages = page_table.shape
    _, _, page, heads, dv = k_cache.shape
    dq = dv // 2
    rows = 2 * heads
    g_pages = _tile(n_pages, g_pages)
    grid_spec = pltpu.PrefetchScalarGridSpec(
        num_scalar_prefetch=1,
        grid=(bsz, n_pages // g_pages),
        in_specs=([_per_seq_spec((rows, dv)), _per_seq_spec((heads, dv)), _per_seq_spec((heads, dv)),
                   _shared_spec((rows, 1)),
                   _shared_spec((1, dq)), _shared_spec((1, dq)), _shared_spec((1, dq)), _shared_spec((1, dq)),
                   _shared_spec((1, dv))]
                  + _page_specs(n_pages, g_pages, (page, heads, dv), layer)
                  + _page_specs(n_pages, g_pages, (page, heads, dv), layer)),
        out_specs=_per_seq_spec((heads, dv)),
        scratch_shapes=[pltpu.VMEM((rows, 1), F32), pltpu.VMEM((rows, 1), F32),
                        pltpu.VMEM((rows, dv), F32)])
    return pl.pallas_call(
        functools.partial(_diff_dec_kernel, g_pages=g_pages, lambda_init=lambda_init, n_pages=n_pages),
        grid_spec=grid_spec,
        out_shape=jax.ShapeDtypeStruct((bsz, heads, dv), BF16),
        compiler_params=_cparams(("parallel", "arbitrary")),
        name="diff_decode",
    )(page_table, q, kn, vn, slopes, lq1, lk1, lq2, lk2, g, *([k_cache] * g_pages), *([v_cache] * g_pages))


def _even_layer(xpb, xsb, caches, page_table, w_in, b_f, w_o, dims):
    cache_k_a, cache_v_a, cache_lf, cache_k_b, cache_v_b, layer_idx = caches
    bsz, t, dbs, heads, dh = dims
    w = heads * dh
    scale = dh ** -0.5
    wb = w_in.astype(BF16)
    wb_b = wb[:, 3 * w + heads:]
    w_f = jnp.pad(wb[:, 3 * w:3 * w + heads], ((0, 0), (0, LANES - heads)))
    b_pad = jnp.pad(b_f.astype(F32), (0, LANES - heads)).reshape(1, LANES)
    wo_a, wo_b = w_o[:w].astype(BF16), w_o[w:].astype(BF16)

    def project(xb, q_dtype, q_scale):
        (q_a,) = _mm(xb, wb, (q_dtype,), out_scale=q_scale, col0=0, n=w)
        k_a, k_ab = _mm(xb, wb, (F32, BF16), col0=w, n=w)
        v_a, v_ab = _mm(xb, wb, (F32, BF16), col0=2 * w, n=w)
        (q_b,) = _mm(xb, wb_b, (q_dtype,), out_scale=q_scale, col0=0, n=w)
        k_b, k_bb = _mm(xb, wb_b, (F32, BF16), col0=w, n=w)
        v_b, v_bb = _mm(xb, wb_b, (F32, BF16), col0=2 * w, n=w)
        lf = _forget_gate(xb, w_f, b_pad)[:, :heads]
        return (q_a, k_a, k_ab, v_a, v_ab, q_b, k_b, k_bb, v_b, v_bb, lf)

    q_a, k_a, k_ab, v_a, v_ab, q_b, k_b, k_bb, v_b, v_bb, lf = project(xpb, BF16, scale * LOG2E)
    lf_t = jnp.swapaxes(lf.reshape(bsz, t, heads), 1, 2)
    cum = _suffix_sums(lf_t.reshape(bsz * heads, t), LOG2E).reshape(bsz, heads, 1, t)
    o_a = _fox_attention(q_a, k_ab, v_ab, cum, bsz, heads, dh)
    o_b = _sb_attention(q_b, k_bb, v_bb, bsz, heads, dh)
    new_p = (k_a.reshape(bsz, t, heads, dh), v_a.reshape(bsz, t, heads, dh), lf.reshape(bsz, t, heads),
             k_b.reshape(bsz, t, heads, dh), v_b.reshape(bsz, t, heads, dh))

    sq_a, sk_a, _, sv_a, _, sq_b, sk_b, _, sv_b, _, slf = project(xsb, F32, scale)
    per_head = lambda a: a.reshape(dbs, heads, dh)
    n_l, n_pool, page, _ = cache_lf.shape
    lf_flat = cache_lf.reshape(n_l, n_pool, 1, page * heads)
    so_a = _fox_decode(page_table, per_head(sq_a), per_head(sk_a), per_head(sv_a), slf.reshape(dbs, heads, 1),
                       lf_flat, cache_k_a, cache_v_a, layer_idx)
    so_b = _sb_decode(page_table, per_head(sq_b), cache_k_b, cache_v_b, layer_idx)
    new_s = (sk_a.reshape(dbs, 1, heads, dh), sv_a.reshape(dbs, 1, heads, dh), slf.reshape(dbs, 1, heads),
             sk_b.reshape(dbs, 1, heads, dh), sv_b.reshape(dbs, 1, heads, dh))
    return (o_a, o_b), (so_a.reshape(dbs, w), so_b.reshape(dbs, w)), (wo_a, wo_b), new_p, new_s


def _odd_layer(xpb, xsb, caches, page_table, w_in, lq1, lk1, lq2, lk2, gain, w_o, lambda_init, dims):
    cache_k, cache_v, layer_idx = caches
    bsz, t, dbs, heads, dq = dims
    dv = 2 * dq
    w = heads * dv
    scale = dq ** -0.5
    wb = w_in.astype(BF16)
    slopes = 2.0 ** (-8.0 * jnp.arange(1, heads + 1, dtype=F32) / heads)
    vecs = [a.astype(F32).reshape(1, dq) for a in (lq1, lk1, lq2, lk2)]
    g_row = gain.astype(F32).reshape(1, dv)

    def project(xb, q_dtype, q_scale):
        (q,) = _mm(xb, wb, (q_dtype,), out_scale=q_scale, col0=0, n=w)
        k, kb = _mm(xb, wb, (F32, BF16), col0=w, n=w)
        v, vb = _mm(xb, wb, (F32, BF16), col0=2 * w, n=w)
        return q, k, kb, v, vb

    q, k, kb, v, vb = project(xpb, BF16, scale * LOG2E)
    o = _diff_attention(q, kb, vb, slopes * LOG2E, *vecs, g_row, bsz, heads, dq, lambda_init)
    new_p = (k.reshape(bsz, t, heads, dv), v.reshape(bsz, t, heads, dv))

    sq, sk, _, sv, _ = project(xsb, F32, scale)
    sq4 = sq.reshape(dbs, heads, 2, dq)
    zero = jnp.zeros((dbs, heads, dq), F32)
    q_rows = jnp.concatenate([jnp.concatenate([sq4[:, :, 0], zero], axis=2),
                              jnp.concatenate([zero, sq4[:, :, 1]], axis=2)], axis=1)
    slope_rows = jnp.concatenate([slopes, slopes]).reshape(2 * heads, 1)
    so = _diff_decode(page_table, q_rows, sk.reshape(dbs, heads, dv), sv.reshape(dbs, heads, dv), slope_rows,
                      *vecs, g_row, cache_k, cache_v, layer_idx, lambda_init)
    new_s = (sk.reshape(dbs, 1, heads, dv), sv.reshape(dbs, 1, heads, dv))
    return (o,), (so.reshape(dbs, w),), (w_o.astype(BF16),), new_p, new_s


def kernel(x_prompt, x_sample, cache_fox_k, cache_fox_v, cache_fox_logf, cache_sb_k, cache_sb_v, cache_diff_k, cache_diff_v, page_table, w_in_even, b_forget, w_out_even, w_in_odd, lambda_q1, lambda_k1, lambda_q2, lambda_k2, subln_gain, w_out_odd, ln_mix_gain, ln_mix_bias, ln_ffn_gain, ln_ffn_bias, w_ffn_up, w_ffn_down):
    bsz, t, d = x_prompt.shape
    dbs, ts, _ = x_sample.shape
    assert ts == 1, "the sample group carries one new token per sequence"
    depth = ln_mix_gain.shape[0]
    alpha = (2 * depth) ** 0.25
    h_fox, dh = cache_fox_k.shape[3], cache_fox_k.shape[4]
    h_diff, dq = cache_diff_k.shape[3], cache_diff_k.shape[4] // 2
    assert cache_sb_k.shape[3:] == (h_fox, dh), "both head groups of an even layer share one geometry"

    xp = x_prompt.reshape(bsz * t, d)
    xs = x_sample.reshape(dbs, d)
    xpb, xsb = xp.astype(BF16), xs.astype(BF16)
    even_p, even_s, odd_p, odd_s = [], [], [], []
    row = lambda a: a.astype(F32).reshape(1, d)
    w_up, w_down = w_ffn_up.astype(BF16), w_ffn_down.astype(BF16)
    for layer in range(depth):
        i = layer // 2
        if layer % 2 == 0:
            caches = (cache_fox_k, cache_fox_v, cache_fox_logf, cache_sb_k, cache_sb_v, i)
            o_p, o_s, w_o, new_p, new_s = _even_layer(xpb, xsb, caches, page_table, w_in_even[i],
                                                      b_forget[i], w_out_even[i], (bsz, t, dbs, h_fox, dh))
            even_p.append(new_p)
            even_s.append(new_s)
        else:
            lambda_init = 0.8 - 0.6 * math.exp(-0.3 * layer)
            caches = (cache_diff_k, cache_diff_v, i)
            o_p, o_s, w_o, new_p, new_s = _odd_layer(xpb, xsb, caches, page_table, w_in_odd[i], lambda_q1[i],
                                                     lambda_k1[i], lambda_q2[i], lambda_k2[i], subln_gain[i],
                                                     w_out_odd[i], lambda_init, (bsz, t, dbs, h_diff, dq))
            odd_p.append(new_p)
            odd_s.append(new_s)
        g1, b1 = row(ln_mix_gain[layer]), row(ln_mix_bias[layer])
        g2, b2 = row(ln_ffn_gain[layer]), row(ln_ffn_bias[layer])
        xp, xpb = _proj_residual_ln(list(o_p), list(w_o), xp, g1, b1, alpha)
        xs, xsb = _proj_residual_ln(list(o_s), list(w_o), xs, g1, b1, alpha)
        xp, xpb = _ffn_residual_ln(xpb, xp, w_up, w_down, layer, g2, b2, alpha)
        xs, xsb = _ffn_residual_ln(xsb, xs, w_up, w_down, layer, g2, b2, alpha)

    def stk(rows, j):
        return jnp.stack([r[j] for r in rows])

    return (xp.reshape(bsz, t, d), xs.reshape(dbs, 1, d),
            stk(even_p, 0), stk(even_p, 1), stk(even_p, 2), stk(even_p, 3), stk(even_p, 4), stk(odd_p, 0), stk(odd_p, 1),
            stk(even_s, 0), stk(even_s, 1), stk(even_s, 2), stk(even_s, 3), stk(even_s, 4), stk(odd_s, 0), stk(odd_s, 1))
```

```python
import functools
import math

import jax
import jax.numpy as jnp
from jax import lax
from jax.experimental import pallas as pl
from jax.experimental.pallas import tpu as pltpu

F32 = jnp.float32
BF16 = jnp.bfloat16
LN_EPS = 1e-5
NEG = -1e30
LOG2E = 1.4426950408889634
LANES = 128
CUM_CHUNK = 256
DIAG_STRIP = 256
VMEM_LIMIT = 56 * 1024 * 1024


def _cparams(sem):
    return pltpu.CompilerParams(dimension_semantics=sem, vmem_limit_bytes=VMEM_LIMIT)


def _tile(n, pref):
    t = min(n, pref)
    while n % t:
        t //= 2
    return t


def _dot(a, b):
    return jnp.dot(a, b, preferred_element_type=F32)


def _dot_nt(a, b):
    return lax.dot_general(a, b, (((1,), (1,)), ((), ())), preferred_element_type=F32)


def _bf16_terms(x, terms):
    out = []
    r = x
    for t in range(terms):
        h = r.astype(BF16)
        out.append(h)
        if t + 1 < terms:
            r = r - h.astype(F32)
    return out


def _split_dot(x, u, terms):
    out = None
    for h in _bf16_terms(x, terms):
        d = _dot(h, u)
        out = d if out is None else out + d
    return out


def _suffix_matrix(n):
    r = lax.broadcasted_iota(jnp.int32, (n, n), 0)
    c = lax.broadcasted_iota(jnp.int32, (n, n), 1)
    return jnp.where(r > c, 1.0, 0.0).astype(BF16)


def _log_sigmoid(x):
    return jnp.minimum(x, 0.0) - jnp.log1p(jnp.exp(-jnp.abs(x)))


def _layer_norm(x, g, b):
    mu = jnp.mean(x, axis=-1, keepdims=True)
    xc = x - mu
    var = jnp.mean(xc * xc, axis=-1, keepdims=True)
    return xc * lax.rsqrt(var + LN_EPS) * g + b


def _mm_kernel(x_ref, w_ref, *o_refs, out_scale):
    acc = _dot(x_ref[...], w_ref[...])
    if out_scale != 1.0:
        acc = acc * out_scale
    for o_ref in o_refs:
        o_ref[...] = acc.astype(o_ref.dtype)


def _mm(x, w, out_dtypes, out_scale=1.0, col0=0, n=None, tm=1024, tn=1024):
    m, k = x.shape
    n = w.shape[1] - col0 if n is None else n
    tm, tn = _tile(m, tm), _tile(n, tn)
    assert col0 % tn == 0, "the column window must start on a column-block boundary"
    j0 = col0 // tn
    outs = pl.pallas_call(
        functools.partial(_mm_kernel, out_scale=out_scale),
        grid=(n // tn, m // tm),
        in_specs=[pl.BlockSpec((tm, k), lambda j, i: (i, 0)),
                  pl.BlockSpec((k, tn), lambda j, i: (0, j0 + j))],
        out_specs=[pl.BlockSpec((tm, tn), lambda j, i: (i, j)) for _ in out_dtypes],
        out_shape=[jax.ShapeDtypeStruct((m, n), d) for d in out_dtypes],
        compiler_params=_cparams(("parallel", "parallel")),
        name="proj",
    )(x, w)
    return outs


def _gate_kernel(x_ref, w_ref, b_ref, o_ref):
    o_ref[...] = _log_sigmoid(_dot(x_ref[...], w_ref[...]) + b_ref[...])


def _forget_gate(x, w, b, tm=512):
    m, k = x.shape
    n = w.shape[1]
    tm = _tile(m, tm)
    return pl.pallas_call(
        _gate_kernel,
        grid=(m // tm,),
        in_specs=[pl.BlockSpec((tm, k), lambda i: (i, 0)),
                  pl.BlockSpec((k, n), lambda i: (0, 0)),
                  pl.BlockSpec((1, n), lambda i: (0, 0))],
        out_specs=pl.BlockSpec((tm, n), lambda i: (i, 0)),
        out_shape=jax.ShapeDtypeStruct((m, n), F32),
        compiler_params=_cparams(("parallel",)),
        name="forget_gate",
    )(x, w, b)


def _revcum_kernel(x_ref, o_ref, *, out_scale):
    rows, t = x_ref.shape
    ch = min(CUM_CHUNK, t)
    u = _suffix_matrix(ch)
    carry = jnp.zeros((rows, 1), F32)
    for c in reversed(range(t // ch)):
        xc = x_ref[:, c * ch:(c + 1) * ch]
        o_ref[:, c * ch:(c + 1) * ch] = (_split_dot(xc, u, 3) + carry) * out_scale
        carry = carry + jnp.sum(xc, axis=1, keepdims=True)


def _suffix_sums(x, out_scale):
    return pl.pallas_call(
        functools.partial(_revcum_kernel, out_scale=out_scale),
        out_shape=jax.ShapeDtypeStruct(x.shape, F32),
        name="suffix_sums",
    )(x)


def _proj_ln_kernel(*refs, n_pairs, alpha):
    a_refs = refs[:n_pairs]
    w_refs = refs[n_pairs:2 * n_pairs]
    x_ref, g_ref, b_ref, y_ref, yb_ref = refs[2 * n_pairs:]
    acc = alpha * x_ref[...]
    for a_ref, w_ref in zip(a_refs, w_refs):
        acc = acc + _dot(a_ref[...], w_ref[...])
    y = _layer_norm(acc, g_ref[...], b_ref[...])
    y_ref[...] = y
    yb_ref[...] = y.astype(BF16)


def _proj_residual_ln(a_list, w_list, x, g, b, alpha, tm=512):
    m, d = x.shape
    tm = _tile(m, tm)
    n_pairs = len(a_list)
    in_specs = ([pl.BlockSpec((tm, a.shape[1]), lambda i: (i, 0)) for a in a_list]
                + [pl.BlockSpec(w.shape, lambda i: (0, 0)) for w in w_list]
                + [pl.BlockSpec((tm, d), lambda i: (i, 0)),
                   pl.BlockSpec((1, d), lambda i: (0, 0)),
                   pl.BlockSpec((1, d), lambda i: (0, 0))])
    return pl.pallas_call(
        functools.partial(_proj_ln_kernel, n_pairs=n_pairs, alpha=alpha),
        grid=(m // tm,),
        in_specs=in_specs,
        out_specs=[pl.BlockSpec((tm, d), lambda i: (i, 0)), pl.BlockSpec((tm, d), lambda i: (i, 0))],
        out_shape=[jax.ShapeDtypeStruct((m, d), F32), jax.ShapeDtypeStruct((m, d), BF16)],
        compiler_params=_cparams(("parallel",)),
        name="out_proj_ln",
    )(*a_list, *w_list, x, g, b)


def _ffn_step(f, n_f, xb_ref, x_ref, wu, wd, g_ref, b_ref, y_ref, yb_ref, acc_ref, alpha):
    @pl.when(f == 0)
    def _():
        acc_ref[...] = alpha * x_ref[...]

    h = jnp.maximum(_dot(xb_ref[...], wu), 0.0)
    acc_ref[...] += _dot((h * h).astype(BF16), wd)

    @pl.when(f == n_f - 1)
    def _():
        y = _layer_norm(acc_ref[...], g_ref[...], b_ref[...])
        y_ref[...] = y
        yb_ref[...] = y.astype(BF16)


def _ffn_kernel(xb_ref, x_ref, wu_ref, wd_ref, g_ref, b_ref, y_ref, yb_ref, acc_ref, *, alpha):
    _ffn_step(pl.program_id(1), pl.num_programs(1), xb_ref, x_ref, wu_ref[...], wd_ref[...], g_ref, b_ref,
              y_ref, yb_ref, acc_ref, alpha)


def _ffn_residual_ln(xb, x, w_up, w_down, g, b, alpha, tm=512, tf=1024):
    m, d = x.shape
    dff = w_up.shape[1]
    tm, tf = _tile(m, tm), _tile(dff, tf)
    return pl.pallas_call(
        functools.partial(_ffn_kernel, alpha=alpha),
        grid=(m // tm, dff // tf),
        in_specs=[pl.BlockSpec((tm, d), lambda i, f: (i, 0)),
                  pl.BlockSpec((tm, d), lambda i, f: (i, 0)),
                  pl.BlockSpec((d, tf), lambda i, f: (0, f)),
                  pl.BlockSpec((tf, d), lambda i, f: (f, 0)),
                  pl.BlockSpec((1, d), lambda i, f: (0, 0)),
                  pl.BlockSpec((1, d), lambda i, f: (0, 0))],
        out_specs=[pl.BlockSpec((tm, d), lambda i, f: (i, 0)), pl.BlockSpec((tm, d), lambda i, f: (i, 0))],
        out_shape=[jax.ShapeDtypeStruct((m, d), F32), jax.ShapeDtypeStruct((m, d), BF16)],
        scratch_shapes=[pltpu.VMEM((tm, d), F32)],
        compiler_params=_cparams(("parallel", "arbitrary")),
        name="ffn_ln",
    )(xb, x, w_up, w_down, g, b)


def _ffn_cast_kernel(xb_ref, x_ref, wu_ref, wd_ref, g_ref, b_ref, y_ref, yb_ref, wub_ref, wdb_ref, acc_ref, *, alpha):
    wu = wu_ref[...].astype(BF16)
    wd = wd_ref[...].astype(BF16)
    wub_ref[...] = wu
    wdb_ref[...] = wd
    _ffn_step(pl.program_id(0), pl.num_programs(0), xb_ref, x_ref, wu, wd, g_ref, b_ref, y_ref, yb_ref, acc_ref,
              alpha)


def _ffn_residual_ln_casting(xb, x, w_up, w_down, layer, g, b, alpha, tf=512):
    m, d = x.shape
    dff = w_up.shape[2]
    tf = _tile(dff, tf)
    return pl.pallas_call(
        functools.partial(_ffn_cast_kernel, alpha=alpha),
        grid=(dff // tf,),
        in_specs=[pl.BlockSpec((m, d), lambda f: (0, 0)),
                  pl.BlockSpec((m, d), lambda f: (0, 0)),
                  pl.BlockSpec((None, d, tf), lambda f: (layer, 0, f)),
                  pl.BlockSpec((None, tf, d), lambda f: (layer, f, 0)),
                  pl.BlockSpec((1, d), lambda f: (0, 0)),
                  pl.BlockSpec((1, d), lambda f: (0, 0))],
        out_specs=[pl.BlockSpec((m, d), lambda f: (0, 0)), pl.BlockSpec((m, d), lambda f: (0, 0)),
                   pl.BlockSpec((d, tf), lambda f: (0, f)), pl.BlockSpec((tf, d), lambda f: (f, 0))],
        out_shape=[jax.ShapeDtypeStruct((m, d), F32), jax.ShapeDtypeStruct((m, d), BF16),
                   jax.ShapeDtypeStruct((d, dff), BF16), jax.ShapeDtypeStruct((dff, d), BF16)],
        scratch_shapes=[pltpu.VMEM((m, d), F32)],
        compiler_params=_cparams(("arbitrary",)),
        name="ffn_ln_casting",
    )(xb, x, w_up, w_down, g, b)


def _pair_tables(n_blocks, descending):
    qi, kj = [], []
    for q in range(n_blocks):
        ks = range(q, -1, -1) if descending else range(q + 1)
        for k in ks:
            qi.append(q)
            kj.append(k)
    return jnp.asarray(qi, jnp.int32), jnp.asarray(kj, jnp.int32)


def _tri_iota(t):
    r = lax.broadcasted_iota(jnp.int32, (t, t), 0)
    c = lax.broadcasted_iota(jnp.int32, (t, t), 1)
    return r, c


def _causal(rows, width, row0=0, col0=0, strict=False):
    r = jnp.tile(lax.broadcasted_iota(jnp.int32, (rows, LANES), 0), (1, width // LANES)) + row0
    c = lax.broadcasted_iota(jnp.int32, (1, width), 1) + col0
    return (c < r) if strict else (c <= r)


def _row_strips(t, diagonal, rows=DIAG_STRIP):
    if not diagonal:
        return [(0, t, t)]
    rows = min(rows, t)
    return [(r0, r0 + rows, r0 + rows) for r0 in range(0, t, rows)]


def _lanes_to_rows(row, t):
    r, c = _tri_iota(t)
    col = jnp.sum(jnp.where(r == c, row, 0.0), axis=1, keepdims=True)
    return jnp.broadcast_to(col, (t, LANES))


def _softmax_step(a, row_shift, m_prev, l_prev, acc_prev, v):
    t, tk = a.shape
    m_next = jnp.maximum(m_prev, jnp.max(a, axis=1, keepdims=True) - row_shift)
    pr = jnp.exp2(a - jnp.tile(m_next + row_shift, (1, tk // LANES)))
    alpha = jnp.exp2(m_prev - m_next)
    l_next = alpha * l_prev + jnp.sum(pr, axis=1, keepdims=True)
    acc_next = jnp.tile(alpha, (1, v.shape[1] // LANES)) * acc_prev + _dot(pr.astype(BF16), v)
    return m_next, l_next, acc_next


def _fox_kernel(qi_tab, kj_tab, q_ref, k_ref, v_ref, ck_ref, cqrow_ref, o_ref, m_s, l_s, acc_s, cq_s, *, dh):
    p = pl.program_id(2)
    qi, kj = qi_tab[p], kj_tab[p]
    t = q_ref.shape[0]
    hb = m_s.shape[0]

    @pl.when(kj == 0)
    def _():
        m_s[...] = jnp.full(m_s.shape, -jnp.inf, F32)
        l_s[...] = jnp.zeros(l_s.shape, F32)
        acc_s[...] = jnp.zeros(acc_s.shape, F32)
        for i in range(hb):
            cq_s[i] = _lanes_to_rows(cqrow_ref[i], t)

    def step(diagonal):
        for i in range(hb):
            cols = slice(i * dh, (i + 1) * dh)
            for r0, r1, kc in _row_strips(t, diagonal):
                a = _dot_nt(q_ref[r0:r1, cols], k_ref[:kc, cols]) + ck_ref[i, :, :kc]
                if diagonal:
                    a = jnp.where(_causal(r1 - r0, kc, row0=r0), a, NEG)
                m, l, acc = _softmax_step(a, cq_s[i, r0:r1], m_s[i, r0:r1], l_s[i, r0:r1], acc_s[i, r0:r1],
                                          v_ref[:kc, cols])
                m_s[i, r0:r1] = m
                l_s[i, r0:r1] = l
                acc_s[i, r0:r1] = acc

    @pl.when(kj < qi)
    def _():
        step(False)

    @pl.when(kj == qi)
    def _():
        step(True)
        for i in range(hb):
            o_ref[:, i * dh:(i + 1) * dh] = (acc_s[i] / l_s[i]).astype(o_ref.dtype)


def _fox_attention(q, k, v, cum, batch, heads, dh, tb=512, hb=4):
    assert dh == LANES
    m = q.shape[0]
    t = m // batch
    tb = _tile(t, tb)
    hb = _tile(heads, hb)
    nb = t // tb
    qi_tab, kj_tab = _pair_tables(nb, descending=False)
    grid_spec = pltpu.PrefetchScalarGridSpec(
        num_scalar_prefetch=2,
        grid=(batch, heads // hb, int(qi_tab.shape[0])),
        in_specs=[pl.BlockSpec((tb, hb * dh), lambda b, h, p, qt, kt: (b * nb + qt[p], h)),
                  pl.BlockSpec((tb, hb * dh), lambda b, h, p, qt, kt: (b * nb + kt[p], h)),
                  pl.BlockSpec((tb, hb * dh), lambda b, h, p, qt, kt: (b * nb + kt[p], h)),
                  pl.BlockSpec((None, hb, 1, tb), lambda b, h, p, qt, kt: (b, h, 0, kt[p])),
                  pl.BlockSpec((None, hb, 1, tb), lambda b, h, p, qt, kt: (b, h, 0, qt[p]))],
        out_specs=pl.BlockSpec((tb, hb * dh), lambda b, h, p, qt, kt: (b * nb + qt[p], h)),
        scratch_shapes=[pltpu.VMEM((hb, tb, LANES), F32), pltpu.VMEM((hb, tb, LANES), F32),
                        pltpu.VMEM((hb, tb, dh), F32), pltpu.VMEM((hb, tb, LANES), F32)])
    return pl.pallas_call(
        functools.partial(_fox_kernel, dh=dh),
        grid_spec=grid_spec,
        out_shape=jax.ShapeDtypeStruct((m, heads * dh), BF16),
        compiler_params=_cparams(("parallel", "parallel", "arbitrary")),
        name="fox_attention",
    )(qi_tab, kj_tab, q, k, v, cum, cum)


def _sb_kernel(qi_tab, kj_tab, q_ref, k_ref, v_ref, o_ref, carry_s, acc_s, *, dh):
    p = pl.program_id(2)
    qi, kj = qi_tab[p], kj_tab[p]
    t = q_ref.shape[0]
    ch = min(CUM_CHUNK, t)
    hb = acc_s.shape[0]

    @pl.when(kj == qi)
    def _():
        carry_s[...] = jnp.zeros(carry_s.shape, F32)
        acc_s[...] = jnp.zeros(acc_s.shape, F32)

    def step(diagonal):
        u = _suffix_matrix(ch)
        for i in range(hb):
            cols = slice(i * dh, (i + 1) * dh)
            for r0, r1, kc in _row_strips(t, diagonal, ch):
                drop = carry_s[i, r0:r1]
                acc = acc_s[i, r0:r1]
                for c0 in reversed(range(0, kc, ch)):
                    z = _dot_nt(q_ref[r0:r1, cols], k_ref[c0:c0 + ch, cols])
                    sp = jnp.maximum(z, 0.0) + jnp.log2(1.0 + jnp.exp2(-jnp.abs(z)))
                    on_diagonal = diagonal and c0 + ch > r0
                    if on_diagonal:
                        strict = _causal(r1 - r0, ch, row0=r0, col0=c0, strict=True)
                        sp = jnp.where(strict, sp, 0.0)
                    hi, lo = _bf16_terms(sp, 2)
                    a = jnp.exp2(z - sp - (_dot(hi, u) + _dot(lo, u)) - jnp.tile(drop, (1, ch // LANES)))
                    if on_diagonal:
                        a = jnp.where(strict, a, 0.0)
                    acc = acc + _dot(a.astype(BF16), v_ref[c0:c0 + ch, cols])
                    drop = drop + jnp.sum(sp, axis=1, keepdims=True)
                carry_s[i, r0:r1] = drop
                acc_s[i, r0:r1] = acc

    @pl.when(kj == qi)
    def _():
        step(True)

    @pl.when(kj < qi)
    def _():
        step(False)

    @pl.when(kj == 0)
    def _():
        for i in range(hb):
            o_ref[:, i * dh:(i + 1) * dh] = acc_s[i].astype(o_ref.dtype)


def _sb_attention(q, k, v, batch, heads, dh, tb=512, hb=4):
    m = q.shape[0]
    t = m // batch
    tb = _tile(t, tb)
    hb = _tile(heads, hb)
    nb = t // tb
    qi_tab, kj_tab = _pair_tables(nb, descending=True)
    grid_spec = pltpu.PrefetchScalarGridSpec(
        num_scalar_prefetch=2,
        grid=(batch, heads // hb, int(qi_tab.shape[0])),
        in_specs=[pl.BlockSpec((tb, hb * dh), lambda b, h, p, qt, kt: (b * nb + qt[p], h)),
                  pl.BlockSpec((tb, hb * dh), lambda b, h, p, qt, kt: (b * nb + kt[p], h)),
                  pl.BlockSpec((tb, hb * dh), lambda b, h, p, qt, kt: (b * nb + kt[p], h))],
        out_specs=pl.BlockSpec((tb, hb * dh), lambda b, h, p, qt, kt: (b * nb + qt[p], h)),
        scratch_shapes=[pltpu.VMEM((hb, tb, LANES), F32), pltpu.VMEM((hb, tb, dh), F32)])
    return pl.pallas_call(
        functools.partial(_sb_kernel, dh=dh),
        grid_spec=grid_spec,
        out_shape=jax.ShapeDtypeStruct((m, heads * dh), BF16),
        compiler_params=_cparams(("parallel", "parallel", "arbitrary")),
        name="sb_attention",
    )(qi_tab, kj_tab, q, k, v)


def _lambda_value(lq1_ref, lk1_ref, lq2_ref, lk2_ref, lambda_init):
    a = jnp.sum(lq1_ref[...] * lk1_ref[...], axis=1, keepdims=True)
    b = jnp.sum(lq2_ref[...] * lk2_ref[...], axis=1, keepdims=True)
    return jnp.exp(a) - jnp.exp(b) + lambda_init


def _diff_kernel(qi_tab, kj_tab, slope_ref, q_ref, k_ref, v_ref, lq1_ref, lk1_ref, lq2_ref, lk2_ref,
                 g_ref, o_ref, m_s, l_s, acc_s, *, lambda_init, dq):
    hg = pl.program_id(1)
    p = pl.program_id(2)
    qi, kj = qi_tab[p], kj_tab[p]
    t = q_ref.shape[0]
    dv = 2 * dq
    hb = m_s.shape[0] // 2

    @pl.when(kj == 0)
    def _():
        m_s[...] = jnp.full(m_s.shape, -jnp.inf, F32)
        l_s[...] = jnp.zeros(l_s.shape, F32)
        acc_s[...] = jnp.zeros(acc_s.shape, F32)

    def step(diagonal):
        for j in range(hb):
            slope = slope_ref[hg * hb + j]
            col_bias = slope * lax.broadcasted_iota(jnp.int32, (1, t), 1).astype(F32)
            row_pos = (qi - kj) * t + lax.broadcasted_iota(jnp.int32, (t, LANES), 0)
            row_shift = slope * row_pos.astype(F32)
            for i in range(2):
                cols = slice(j * dv + i * dq, j * dv + (i + 1) * dq)
                s = 2 * j + i
                for r0, r1, kc in _row_strips(t, diagonal):
                    a = _dot_nt(q_ref[r0:r1, cols], k_ref[:kc, cols]) + col_bias[:, :kc]
                    if diagonal:
                        a = jnp.where(_causal(r1 - r0, kc, row0=r0), a, NEG)
                    m, l, acc = _softmax_step(a, row_shift[r0:r1], m_s[s, r0:r1], l_s[s, r0:r1], acc_s[s, r0:r1],
                                              v_ref[:kc, j * dv:(j + 1) * dv])
                    m_s[s, r0:r1] = m
                    l_s[s, r0:r1] = l
                    acc_s[s, r0:r1] = acc

    @pl.when(kj < qi)
    def _():
        step(False)

    @pl.when(kj == qi)
    def _():
        step(True)
        lam = _lambda_value(lq1_ref, lk1_ref, lq2_ref, lk2_ref, lambda_init)
        reps = dv // LANES
        for j in range(hb):
            o = (acc_s[2 * j] / jnp.tile(l_s[2 * j], (1, reps))
                 - lam * (acc_s[2 * j + 1] / jnp.tile(l_s[2 * j + 1], (1, reps))))
            o = o * lax.rsqrt(jnp.mean(o * o, axis=1, keepdims=True) + LN_EPS) * g_ref[...]
            o_ref[:, j * dv:(j + 1) * dv] = (o * (1.0 - lambda_init)).astype(o_ref.dtype)


def _diff_attention(q, k, v, slopes, lq1, lk1, lq2, lk2, g, batch, heads, dq, lambda_init, tb=512, hb=4):
    m = q.shape[0]
    t = m // batch
    tb = _tile(t, tb)
    hb = _tile(heads, hb)
    nb = t // tb
    dv = 2 * dq
    qi_tab, kj_tab = _pair_tables(nb, descending=False)
    vec = pl.BlockSpec((1, dq), lambda b, h, p, qt, kt, sl: (0, 0))
    grid_spec = pltpu.PrefetchScalarGridSpec(
        num_scalar_prefetch=3,
        grid=(batch, heads // hb, int(qi_tab.shape[0])),
        in_specs=[pl.BlockSpec((tb, hb * dv), lambda b, h, p, qt, kt, sl: (b * nb + qt[p], h)),
                  pl.BlockSpec((tb, hb * dv), lambda b, h, p, qt, kt, sl: (b * nb + kt[p], h)),
                  pl.BlockSpec((tb, hb * dv), lambda b, h, p, qt, kt, sl: (b * nb + kt[p], h)),
                  vec, vec, vec, vec,
                  pl.BlockSpec((1, dv), lambda b, h, p, qt, kt, sl: (0, 0))],
        out_specs=pl.BlockSpec((tb, hb * dv), lambda b, h, p, qt, kt, sl: (b * nb + qt[p], h)),
        scratch_shapes=[pltpu.VMEM((2 * hb, tb, LANES), F32), pltpu.VMEM((2 * hb, tb, LANES), F32),
                        pltpu.VMEM((2 * hb, tb, dv), F32)])
    return pl.pallas_call(
        functools.partial(_diff_kernel, lambda_init=lambda_init, dq=dq),
        grid_spec=grid_spec,
        out_shape=jax.ShapeDtypeStruct((m, heads * dv), BF16),
        compiler_params=_cparams(("parallel", "parallel", "arbitrary")),
        name="diff_attention",
    )(qi_tab, kj_tab, slopes, q, k, v, lq1, lk1, lq2, lk2, g)


def _own_head(rows, width, heads):
    assert heads & (heads - 1) == 0
    r = lax.broadcasted_iota(jnp.int32, (rows, width), 0)
    c = lax.broadcasted_iota(jnp.int32, (rows, width), 1)
    return (c & (heads - 1)) == (r & (heads - 1))


def _flat_pages(refs):
    return [ref[...].reshape(ref.shape[0] * ref.shape[1], ref.shape[2]).astype(BF16) for ref in refs]


def _wide_scores(q, k_refs):
    s = [_dot_nt(q, k) for k in _flat_pages(k_refs)]
    return jnp.concatenate(s, axis=1) if len(s) > 1 else s[0]


def _wide_pv(w, v_refs):
    out = None
    width = v_refs[0].shape[0] * v_refs[0].shape[1]
    for g, v in enumerate(_flat_pages(v_refs)):
        d = _dot(w[:, g * width:(g + 1) * width], v)
        out = d if out is None else out + d
    return out


def _wide_suffix_sums(x, carry, terms):
    rows, w = x.shape
    ch = min(CUM_CHUNK, w)
    n = w // ch
    u = _suffix_matrix(ch)
    parts = [p.astype(F32) for p in _bf16_terms(x, terms)]
    stacked = jnp.concatenate([p[:, c * ch:(c + 1) * ch] for c in range(n) for p in parts], axis=0)
    d = _dot(stacked.astype(BF16), u)
    outs = [None] * n
    for c in reversed(range(n)):
        base = c * terms * rows
        tail = d[base:base + rows]
        for t in range(1, terms):
            tail = tail + d[base + t * rows:base + (t + 1) * rows]
        outs[c] = tail + carry
        carry = carry + jnp.sum(x[:, c * ch:(c + 1) * ch], axis=1, keepdims=True)
    return (jnp.concatenate(outs, axis=1) if n > 1 else outs[0]), carry


def _fox_dec_kernel(pt_ref, q_ref, kn_ref, vn_ref, lfn_ref, *refs, g_pages, scale):
    lf_refs = refs[:g_pages]
    k_refs = refs[g_pages:2 * g_pages]
    v_refs = refs[2 * g_pages:3 * g_pages]
    o_ref, m_s, l_s, acc_s, carry_s = refs[3 * g_pages:]
    j = pl.program_id(1)
    heads = q_ref.shape[0]
    q = (q_ref[...] * scale).astype(BF16)

    @pl.when(j == 0)
    def _():
        qn = q.astype(F32) * kn_ref[...].astype(BF16).astype(F32)
        m_s[...] = jnp.sum(qn, axis=1, keepdims=True)
        l_s[...] = jnp.ones(l_s.shape, F32)
        acc_s[...] = vn_ref[...].astype(BF16).astype(F32)
        carry_s[...] = lfn_ref[...]

    s = _wide_scores(q, k_refs)
    own = _own_head(heads, s.shape[1], heads)
    lf = jnp.concatenate([r[...] for r in lf_refs], axis=1) if g_pages > 1 else lf_refs[0][...]
    cum, carry = _wide_suffix_sums(jnp.where(own, lf, 0.0), carry_s[...], 3)
    carry_s[...] = carry
    s = jnp.where(own, s + cum, NEG)
    m_old = m_s[...]
    m_new = jnp.maximum(m_old, jnp.max(s, axis=1, keepdims=True))
    alpha = jnp.exp(m_old - m_new)
    pr = jnp.exp(s - m_new)
    l_s[...] = alpha * l_s[...] + jnp.sum(pr, axis=1, keepdims=True)
    acc_s[...] = alpha * acc_s[...] + _wide_pv(pr.astype(BF16), v_refs)
    m_s[...] = m_new

    @pl.when(j == pl.num_programs(1) - 1)
    def _():
        o_ref[...] = (acc_s[...] / l_s[...]).astype(o_ref.dtype)


def _sb_dec_kernel(pt_ref, q_ref, *refs, g_pages, scale):
    k_refs = refs[:g_pages]
    v_refs = refs[g_pages:2 * g_pages]
    o_ref, acc_s, carry_s = refs[2 * g_pages:]
    j = pl.program_id(1)
    heads = q_ref.shape[0]
    q = (q_ref[...] * scale).astype(BF16)

    @pl.when(j == 0)
    def _():
        acc_s[...] = jnp.zeros(acc_s.shape, F32)
        carry_s[...] = jnp.zeros(carry_s.shape, F32)

    z = _wide_scores(q, k_refs)
    own = _own_head(heads, z.shape[1], heads)
    log_keep = jnp.where(own, _log_sigmoid(-z), 0.0)
    tail, carry = _wide_suffix_sums(log_keep, carry_s[...], 2)
    carry_s[...] = carry
    a = jnp.where(own, jnp.exp(log_keep + z + tail), 0.0)
    acc_s[...] += _wide_pv(a.astype(BF16), v_refs)

    @pl.when(j == pl.num_programs(1) - 1)
    def _():
        o_ref[...] = acc_s[...].astype(o_ref.dtype)


def _diff_dec_kernel(pt_ref, q_ref, kn_ref, vn_ref, slope_ref, lq1_ref, lk1_ref, lq2_ref, lk2_ref,
                     g_ref, *refs, g_pages, scale, lambda_init, n_pages):
    k_refs = refs[:g_pages]
    v_refs = refs[g_pages:2 * g_pages]
    o_ref, m_s, l_s, acc_s = refs[2 * g_pages:]
    j = pl.program_id(1)
    page, heads, _ = k_refs[0].shape
    rows = 2 * heads
    q = (q_ref[...] * scale).astype(BF16)

    @pl.when(j == 0)
    def _():
        kn = kn_ref[...].astype(BF16).astype(F32)
        qn = q.astype(F32) * jnp.concatenate([kn, kn], axis=0)
        m_s[...] = jnp.sum(qn, axis=1, keepdims=True)
        l_s[...] = jnp.ones(l_s.shape, F32)
        vn = vn_ref[...].astype(BF16).astype(F32)
        acc_s[...] = jnp.concatenate([vn, vn], axis=0)

    s = _wide_scores(q, k_refs)
    own = _own_head(rows, s.shape[1], heads)
    first = (n_pages - (j + 1) * g_pages) * page
    col = lax.broadcasted_iota(jnp.int32, s.shape, 1)
    dist = (n_pages * page - first - jnp.right_shift(col, heads.bit_length() - 1)).astype(F32)
    s = jnp.where(own, s - slope_ref[...] * dist, NEG)
    m_old = m_s[...]
    m_new = jnp.maximum(m_old, jnp.max(s, axis=1, keepdims=True))
    alpha = jnp.exp(m_old - m_new)
    pr = jnp.exp(s - m_new)
    l_s[...] = alpha * l_s[...] + jnp.sum(pr, axis=1, keepdims=True)
    acc_s[...] = alpha * acc_s[...] + _wide_pv(pr.astype(BF16), v_refs)
    m_s[...] = m_new

    @pl.when(j == pl.num_programs(1) - 1)
    def _():
        lam = _lambda_value(lq1_ref, lk1_ref, lq2_ref, lk2_ref, lambda_init)
        on = acc_s[...] / l_s[...]
        o = on[:heads] - lam * on[heads:]
        o = o * lax.rsqrt(jnp.mean(o * o, axis=1, keepdims=True) + LN_EPS) * g_ref[...]
        o_ref[...] = (o * (1.0 - lambda_init)).astype(o_ref.dtype)


def _page_specs(n_pages, g_pages, block, layer):
    def make(g):
        def index(b, j, pt):
            return (layer, pt[b, n_pages - (j + 1) * g_pages + g]) + (0,) * len(block)
        return pl.BlockSpec((None, None) + block, index)
    return [make(g) for g in range(g_pages)]


def _per_seq_spec(shape):
    return pl.BlockSpec((None,) + shape, lambda b, j, pt: (b,) + (0,) * len(shape))


def _shared_spec(shape):
    return pl.BlockSpec(shape, lambda b, j, pt: (0,) * len(shape))


def _fox_decode(page_table, q, kn, vn, lfn, lf_cache, k_cache, v_cache, layer, g_pages=16):
    bsz, n_pages = page_table.shape
    _, _, page, heads, dh = k_cache.shape
    g_pages = _tile(n_pages, g_pages)
    grid_spec = pltpu.PrefetchScalarGridSpec(
        num_scalar_prefetch=1,
        grid=(bsz, n_pages // g_pages),
        in_specs=([_per_seq_spec((heads, dh))] * 3 + [_per_seq_spec((heads, 1))]
                  + _page_specs(n_pages, g_pages, (1, page * heads), layer)
                  + _page_specs(n_pages, g_pages, (page, heads, dh), layer)
                  + _page_specs(n_pages, g_pages, (page, heads, dh), layer)),
        out_specs=_per_seq_spec((heads, dh)),
        scratch_shapes=[pltpu.VMEM((heads, 1), F32), pltpu.VMEM((heads, 1), F32),
                        pltpu.VMEM((heads, dh), F32), pltpu.VMEM((heads, 1), F32)])
    return pl.pallas_call(
        functools.partial(_fox_dec_kernel, g_pages=g_pages, scale=dh ** -0.5),
        grid_spec=grid_spec,
        out_shape=jax.ShapeDtypeStruct((bsz, heads, dh), BF16),
        compiler_params=_cparams(("parallel", "arbitrary")),
        name="fox_decode",
    )(page_table, q, kn, vn, lfn, *([lf_cache] * g_pages), *([k_cache] * g_pages), *([v_cache] * g_pages))


def _sb_decode(page_table, q, k_cache, v_cache, layer, g_pages=16):
    bsz, n_pages = page_table.shape
    _, _, page, heads, dh = k_cache.shape
    g_pages = _tile(n_pages, g_pages)
    grid_spec = pltpu.PrefetchScalarGridSpec(
        num_scalar_prefetch=1,
        grid=(bsz, n_pages // g_pages),
        in_specs=([_per_seq_spec((heads, dh))]
                  + _page_specs(n_pages, g_pages, (page, heads, dh), layer)
                  + _page_specs(n_pages, g_pages, (page, heads, dh), layer)),
        out_specs=_per_seq_spec((heads, dh)),
        scratch_shapes=[pltpu.VMEM((heads, dh), F32), pltpu.VMEM((heads, 1), F32)])
    return pl.pallas_call(
        functools.partial(_sb_dec_kernel, g_pages=g_pages, scale=dh ** -0.5),
        grid_spec=grid_spec,
        out_shape=jax.ShapeDtypeStruct((bsz, heads, dh), BF16),
        compiler_params=_cparams(("parallel", "arbitrary")),
        name="sb_decode",
    )(page_table, q, *([k_cache] * g_pages), *([v_cache] * g_pages))


def _diff_decode(page_table, q, kn, vn, slopes, lq1, lk1, lq2, lk2, g, k_cache, v_cache, layer, lambda_init,
                 g_pages=8):
    bsz, n_pages = page_table.shape
    _, _, page, heads, dv = k_cache.shape
    dq = dv // 2
    rows = 2 * heads
    g_pages = _tile(n_pages, g_pages)
    grid_spec = pltpu.PrefetchScalarGridSpec(
        num_scalar_prefetch=1,
        grid=(bsz, n_pages // g_pages),
        in_specs=([_per_seq_spec((rows, dv)), _per_seq_spec((heads, dv)), _per_seq_spec((heads, dv)),
                   _shared_spec((rows, 1)),
                   _shared_spec((1, dq)), _shared_spec((1, dq)), _shared_spec((1, dq)), _shared_spec((1, dq)),
                   _shared_spec((1, dv))]
                  + _page_specs(n_pages, g_pages, (page, heads, dv), layer)
                  + _page_specs(n_pages, g_pages, (page, heads, dv), layer)),
        out_specs=_per_seq_spec((heads, dv)),
        scratch_shapes=[pltpu.VMEM((rows, 1), F32), pltpu.VMEM((rows, 1), F32),
                        pltpu.VMEM((rows, dv), F32)])
    return pl.pallas_call(
        functools.partial(_diff_dec_kernel, g_pages=g_pages, scale=dq ** -0.5, lambda_init=lambda_init,
                          n_pages=n_pages),
        grid_spec=grid_spec,
        out_shape=jax.ShapeDtypeStruct((bsz, heads, dv), BF16),
        compiler_params=_cparams(("parallel", "arbitrary")),
        name="diff_decode",
    )(page_table, q, kn, vn, slopes, lq1, lk1, lq2, lk2, g, *([k_cache] * g_pages), *([v_cache] * g_pages))


def _even_layer(xpb, xsb, caches, page_table, w_in, b_f, w_o, dims):
    cache_k_a, cache_v_a, cache_lf, cache_k_b, cache_v_b, layer_idx = caches
    bsz, t, dbs, heads, dh = dims
    w = heads * dh
    scale = dh ** -0.5
    wb = w_in.astype(BF16)
    wb_b = wb[:, 3 * w + heads:]
    w_f = jnp.pad(wb[:, 3 * w:3 * w + heads], ((0, 0), (0, LANES - heads)))
    b_pad = jnp.pad(b_f.astype(F32), (0, LANES - heads)).reshape(1, LANES)
    wo_a, wo_b = w_o[:w].astype(BF16), w_o[w:].astype(BF16)

    (q_a,) = _mm(xpb, wb, (BF16,), out_scale=scale * LOG2E, col0=0, n=w)
    k_a, k_ab = _mm(xpb, wb, (F32, BF16), col0=w, n=w)
    v_a, v_ab = _mm(xpb, wb, (F32, BF16), col0=2 * w, n=w)
    (q_b,) = _mm(xpb, wb_b, (BF16,), out_scale=scale * LOG2E, col0=0, n=w)
    k_b, k_bb = _mm(xpb, wb_b, (F32, BF16), col0=w, n=w)
    v_b, v_bb = _mm(xpb, wb_b, (F32, BF16), col0=2 * w, n=w)
    lf = _forget_gate(xpb, w_f, b_pad)[:, :heads]
    lf_t = jnp.swapaxes(lf.reshape(bsz, t, heads), 1, 2)
    cum = _suffix_sums(lf_t.reshape(bsz * heads, t), LOG2E).reshape(bsz, heads, 1, t)
    o_a = _fox_attention(q_a, k_ab, v_ab, cum, bsz, heads, dh)
    o_b = _sb_attention(q_b, k_bb, v_bb, bsz, heads, dh)
    new_p = (k_a.reshape(bsz, t, heads, dh), v_a.reshape(bsz, t, heads, dh), lf.reshape(bsz, t, heads),
             k_b.reshape(bsz, t, heads, dh), v_b.reshape(bsz, t, heads, dh))

    (qkv_a,) = _mm(xsb, wb, (F32,), col0=0, n=3 * w)
    (qkv_b,) = _mm(xsb, wb_b, (F32,), col0=0, n=3 * w)
    sq_a, sk_a, sv_a = (qkv_a[:, i * w:(i + 1) * w] for i in range(3))
    sq_b, sk_b, sv_b = (qkv_b[:, i * w:(i + 1) * w] for i in range(3))
    slf = _forget_gate(xsb, w_f, b_pad)[:, :heads]
    per_head = lambda a: a.reshape(dbs, heads, dh)
    n_l, n_pool, page, _ = cache_lf.shape
    lf_flat = cache_lf.reshape(n_l, n_pool, 1, page * heads)
    so_a = _fox_decode(page_table, per_head(sq_a), per_head(sk_a), per_head(sv_a), slf.reshape(dbs, heads, 1),
                       lf_flat, cache_k_a, cache_v_a, layer_idx)
    so_b = _sb_decode(page_table, per_head(sq_b), cache_k_b, cache_v_b, layer_idx)
    new_s = (sk_a.reshape(dbs, 1, heads, dh), sv_a.reshape(dbs, 1, heads, dh), slf.reshape(dbs, 1, heads),
             sk_b.reshape(dbs, 1, heads, dh), sv_b.reshape(dbs, 1, heads, dh))
    return (o_a, o_b), (so_a.reshape(dbs, w), so_b.reshape(dbs, w)), (wo_a, wo_b), new_p, new_s


def _odd_layer(xpb, xsb, caches, page_table, w_in, lq1, lk1, lq2, lk2, gain, w_o, lambda_init, dims):
    cache_k, cache_v, layer_idx = caches
    bsz, t, dbs, heads, dq = dims
    dv = 2 * dq
    w = heads * dv
    scale = dq ** -0.5
    wb = w_in.astype(BF16)
    slopes = 2.0 ** (-8.0 * jnp.arange(1, heads + 1, dtype=F32) / heads)
    vecs = [a.astype(F32).reshape(1, dq) for a in (lq1, lk1, lq2, lk2)]
    g_row = gain.astype(F32).reshape(1, dv)

    (q,) = _mm(xpb, wb, (BF16,), out_scale=scale * LOG2E, col0=0, n=w)
    k, kb = _mm(xpb, wb, (F32, BF16), col0=w, n=w)
    v, vb = _mm(xpb, wb, (F32, BF16), col0=2 * w, n=w)
    o = _diff_attention(q, kb, vb, slopes * LOG2E, *vecs, g_row, bsz, heads, dq, lambda_init)
    new_p = (k.reshape(bsz, t, heads, dv), v.reshape(bsz, t, heads, dv))

    (qkv,) = _mm(xsb, wb, (F32,), col0=0, n=3 * w)
    sq, sk, sv = (qkv[:, i * w:(i + 1) * w] for i in range(3))
    sq4 = sq.reshape(dbs, heads, 2, dq)
    zero = jnp.zeros((dbs, heads, dq), F32)
    q_rows = jnp.concatenate([jnp.concatenate([sq4[:, :, 0], zero], axis=2),
                              jnp.concatenate([zero, sq4[:, :, 1]], axis=2)], axis=1)
    slope_rows = jnp.concatenate([slopes, slopes]).reshape(2 * heads, 1)
    so = _diff_decode(page_table, q_rows, sk.reshape(dbs, heads, dv), sv.reshape(dbs, heads, dv), slope_rows,
                      *vecs, g_row, cache_k, cache_v, layer_idx, lambda_init)
    new_s = (sk.reshape(dbs, 1, heads, dv), sv.reshape(dbs, 1, heads, dv))
    return (o,), (so.reshape(dbs, w),), (w_o.astype(BF16),), new_p, new_s


def kernel(x_prompt, x_sample, cache_fox_k, cache_fox_v, cache_fox_logf, cache_sb_k, cache_sb_v, cache_diff_k, cache_diff_v, page_table, w_in_even, b_forget, w_out_even, w_in_odd, lambda_q1, lambda_k1, lambda_q2, lambda_k2, subln_gain, w_out_odd, ln_mix_gain, ln_mix_bias, ln_ffn_gain, ln_ffn_bias, w_ffn_up, w_ffn_down):
    bsz, t, d = x_prompt.shape
    dbs, ts, _ = x_sample.shape
    assert ts == 1, "the sample group carries one new token per sequence"
    depth = ln_mix_gain.shape[0]
    alpha = (2 * depth) ** 0.25
    h_fox, dh = cache_fox_k.shape[3], cache_fox_k.shape[4]
    h_diff, dq = cache_diff_k.shape[3], cache_diff_k.shape[4] // 2
    assert cache_sb_k.shape[3:] == (h_fox, dh), "both head groups of an even layer share one geometry"

    xp = x_prompt.reshape(bsz * t, d)
    xs = x_sample.reshape(dbs, d)
    xpb, xsb = xp.astype(BF16), xs.astype(BF16)
    even_p, even_s, odd_p, odd_s = [], [], [], []
    row = lambda a: a.astype(F32).reshape(1, d)
    for layer in range(depth):
        i = layer // 2
        if layer % 2 == 0:
            caches = (cache_fox_k, cache_fox_v, cache_fox_logf, cache_sb_k, cache_sb_v, i)
            o_p, o_s, w_o, new_p, new_s = _even_layer(xpb, xsb, caches, page_table, w_in_even[i],
                                                      b_forget[i], w_out_even[i], (bsz, t, dbs, h_fox, dh))
            even_p.append(new_p)
            even_s.append(new_s)
        else:
            lambda_init = 0.8 - 0.6 * math.exp(-0.3 * layer)
            caches = (cache_diff_k, cache_diff_v, i)
            o_p, o_s, w_o, new_p, new_s = _odd_layer(xpb, xsb, caches, page_table, w_in_odd[i], lambda_q1[i],
                                                     lambda_k1[i], lambda_q2[i], lambda_k2[i], subln_gain[i],
                                                     w_out_odd[i], lambda_init, (bsz, t, dbs, h_diff, dq))
            odd_p.append(new_p)
            odd_s.append(new_s)
        g1, b1 = row(ln_mix_gain[layer]), row(ln_mix_bias[layer])
        g2, b2 = row(ln_ffn_gain[layer]), row(ln_ffn_bias[layer])
        xp, xpb = _proj_residual_ln(list(o_p), list(w_o), xp, g1, b1, alpha)
        xs, xsb = _proj_residual_ln(list(o_s), list(w_o), xs, g1, b1, alpha)
        xs, xsb, w_up, w_down = _ffn_residual_ln_casting(xsb, xs, w_ffn_up, w_ffn_down, layer, g2, b2, alpha)
        xp, xpb = _ffn_residual_ln(xpb, xp, w_up, w_down, g2, b2, alpha)

    def stk(rows, j):
        return jnp.stack([r[j] for r in rows])

    return (xp.reshape(bsz, t, d), xs.reshape(dbs, 1, d),
            stk(even_p, 0), stk(even_p, 1), stk(even_p, 2), stk(even_p, 3), stk(even_p, 4), stk(odd_p, 0), stk(odd_p, 1),
            stk(even_s, 0), stk(even_s, 1), stk(even_s, 2), stk(even_s, 3), stk(even_s, 4), stk(odd_s, 0), stk(odd_s, 1))
```

```python
import functools
import math

import jax
import jax.numpy as jnp
from jax import lax
from jax.experimental import pallas as pl
from jax.experimental.pallas import tpu as pltpu

F32 = jnp.float32
BF16 = jnp.bfloat16
LN_EPS = 1e-5
NEG = -1e30
LOG2E = 1.4426950408889634
LANES = 128
CUM_CHUNK = 256
DIAG_STRIP = 256
VMEM_LIMIT = 56 * 1024 * 1024


def _cparams(sem):
    return pltpu.CompilerParams(dimension_semantics=sem, vmem_limit_bytes=VMEM_LIMIT)


def _tile(n, pref):
    t = min(n, pref)
    while n % t:
        t //= 2
    return t


def _dot(a, b):
    return jnp.dot(a, b, preferred_element_type=F32)


def _dot_nt(a, b):
    return lax.dot_general(a, b, (((1,), (1,)), ((), ())), preferred_element_type=F32)


def _bf16_terms(x, terms):
    out = []
    r = x
    for t in range(terms):
        h = r.astype(BF16)
        out.append(h)
        if t + 1 < terms:
            r = r - h.astype(F32)
    return out


def _split_dot(x, u, terms):
    out = None
    for h in _bf16_terms(x, terms):
        d = _dot(h, u)
        out = d if out is None else out + d
    return out


def _suffix_matrix(n):
    r = lax.broadcasted_iota(jnp.int32, (n, n), 0)
    c = lax.broadcasted_iota(jnp.int32, (n, n), 1)
    return jnp.where(r > c, 1.0, 0.0).astype(BF16)


def _log_sigmoid(x):
    return jnp.minimum(x, 0.0) - jnp.log1p(jnp.exp(-jnp.abs(x)))


def _layer_norm(x, g, b):
    mu = jnp.mean(x, axis=-1, keepdims=True)
    xc = x - mu
    var = jnp.mean(xc * xc, axis=-1, keepdims=True)
    return xc * lax.rsqrt(var + LN_EPS) * g + b


def _mm_kernel(x_ref, w_ref, *o_refs, out_scale):
    acc = _dot(x_ref[...], w_ref[...])
    if out_scale != 1.0:
        acc = acc * out_scale
    for o_ref in o_refs:
        o_ref[...] = acc.astype(o_ref.dtype)


def _mm(x, w, out_dtypes, out_scale=1.0, col0=0, n=None, tm=1024, tn=1024):
    m, k = x.shape
    n = w.shape[1] - col0 if n is None else n
    tm, tn = _tile(m, tm), _tile(n, tn)
    assert col0 % tn == 0, "the column window must start on a column-block boundary"
    j0 = col0 // tn
    outs = pl.pallas_call(
        functools.partial(_mm_kernel, out_scale=out_scale),
        grid=(n // tn, m // tm),
        in_specs=[pl.BlockSpec((tm, k), lambda j, i: (i, 0)),
                  pl.BlockSpec((k, tn), lambda j, i: (0, j0 + j))],
        out_specs=[pl.BlockSpec((tm, tn), lambda j, i: (i, j)) for _ in out_dtypes],
        out_shape=[jax.ShapeDtypeStruct((m, n), d) for d in out_dtypes],
        compiler_params=_cparams(("parallel", "parallel")),
        name="proj",
    )(x, w)
    return outs


def _gate_kernel(x_ref, w_ref, b_ref, o_ref):
    o_ref[...] = _log_sigmoid(_dot(x_ref[...], w_ref[...]) + b_ref[...])


def _forget_gate(x, w, b, tm=512):
    m, k = x.shape
    n = w.shape[1]
    tm = _tile(m, tm)
    return pl.pallas_call(
        _gate_kernel,
        grid=(m // tm,),
        in_specs=[pl.BlockSpec((tm, k), lambda i: (i, 0)),
                  pl.BlockSpec((k, n), lambda i: (0, 0)),
                  pl.BlockSpec((1, n), lambda i: (0, 0))],
        out_specs=pl.BlockSpec((tm, n), lambda i: (i, 0)),
        out_shape=jax.ShapeDtypeStruct((m, n), F32),
        compiler_params=_cparams(("parallel",)),
        name="forget_gate",
    )(x, w, b)


def _revcum_kernel(x_ref, o_ref, *, out_scale):
    rows, t = x_ref.shape
    ch = min(CUM_CHUNK, t)
    u = _suffix_matrix(ch)
    carry = jnp.zeros((rows, 1), F32)
    for c in reversed(range(t // ch)):
        xc = x_ref[:, c * ch:(c + 1) * ch]
        o_ref[:, c * ch:(c + 1) * ch] = (_split_dot(xc, u, 3) + carry) * out_scale
        carry = carry + jnp.sum(xc, axis=1, keepdims=True)


def _suffix_sums(x, out_scale):
    return pl.pallas_call(
        functools.partial(_revcum_kernel, out_scale=out_scale),
        out_shape=jax.ShapeDtypeStruct(x.shape, F32),
        name="suffix_sums",
    )(x)


def _proj_ln_kernel(*refs, n_pairs, alpha):
    a_refs = refs[:n_pairs]
    w_refs = refs[n_pairs:2 * n_pairs]
    x_ref, g_ref, b_ref, y_ref, yb_ref = refs[2 * n_pairs:]
    acc = alpha * x_ref[...]
    for a_ref, w_ref in zip(a_refs, w_refs):
        acc = acc + _dot(a_ref[...], w_ref[...])
    y = _layer_norm(acc, g_ref[...], b_ref[...])
    y_ref[...] = y
    yb_ref[...] = y.astype(BF16)


def _proj_residual_ln(a_list, w_list, x, g, b, alpha, tm=512):
    m, d = x.shape
    tm = _tile(m, tm)
    n_pairs = len(a_list)
    in_specs = ([pl.BlockSpec((tm, a.shape[1]), lambda i: (i, 0)) for a in a_list]
                + [pl.BlockSpec(w.shape, lambda i: (0, 0)) for w in w_list]
                + [pl.BlockSpec((tm, d), lambda i: (i, 0)),
                   pl.BlockSpec((1, d), lambda i: (0, 0)),
                   pl.BlockSpec((1, d), lambda i: (0, 0))])
    return pl.pallas_call(
        functools.partial(_proj_ln_kernel, n_pairs=n_pairs, alpha=alpha),
        grid=(m // tm,),
        in_specs=in_specs,
        out_specs=[pl.BlockSpec((tm, d), lambda i: (i, 0)), pl.BlockSpec((tm, d), lambda i: (i, 0))],
        out_shape=[jax.ShapeDtypeStruct((m, d), F32), jax.ShapeDtypeStruct((m, d), BF16)],
        compiler_params=_cparams(("parallel",)),
        name="out_proj_ln",
    )(*a_list, *w_list, x, g, b)


def _ffn_step(f, n_f, xb_ref, x_ref, load_weights, g_ref, b_ref, y_ref, yb_ref, acc_ref, alpha):
    @pl.when(f == 0)
    def _():
        acc_ref[...] = alpha * x_ref[...]

    wu, wd = load_weights()
    h = jnp.maximum(_dot(xb_ref[...], wu), 0.0)
    acc_ref[...] += _dot((h * h).astype(BF16), wd)

    @pl.when(f == n_f - 1)
    def _():
        y = _layer_norm(acc_ref[...], g_ref[...], b_ref[...])
        y_ref[...] = y
        yb_ref[...] = y.astype(BF16)


def _ffn_kernel(xb_ref, x_ref, wu_ref, wd_ref, g_ref, b_ref, y_ref, yb_ref, acc_ref, *, alpha):
    _ffn_step(pl.program_id(1), pl.num_programs(1), xb_ref, x_ref, lambda: (wu_ref[...], wd_ref[...]), g_ref, b_ref,
              y_ref, yb_ref, acc_ref, alpha)


def _ffn_residual_ln(xb, x, w_up, w_down, g, b, alpha, tm=512, tf=1024):
    m, d = x.shape
    dff = w_up.shape[1]
    tm, tf = _tile(m, tm), _tile(dff, tf)
    return pl.pallas_call(
        functools.partial(_ffn_kernel, alpha=alpha),
        grid=(m // tm, dff // tf),
        in_specs=[pl.BlockSpec((tm, d), lambda i, f: (i, 0)),
                  pl.BlockSpec((tm, d), lambda i, f: (i, 0)),
                  pl.BlockSpec((d, tf), lambda i, f: (0, f)),
                  pl.BlockSpec((tf, d), lambda i, f: (f, 0)),
                  pl.BlockSpec((1, d), lambda i, f: (0, 0)),
                  pl.BlockSpec((1, d), lambda i, f: (0, 0))],
        out_specs=[pl.BlockSpec((tm, d), lambda i, f: (i, 0)), pl.BlockSpec((tm, d), lambda i, f: (i, 0))],
        out_shape=[jax.ShapeDtypeStruct((m, d), F32), jax.ShapeDtypeStruct((m, d), BF16)],
        scratch_shapes=[pltpu.VMEM((tm, d), F32)],
        compiler_params=_cparams(("parallel", "arbitrary")),
        name="ffn_ln",
    )(xb, x, w_up, w_down, g, b)


def _ffn_cast_kernel(xb_ref, x_ref, wu_ref, wd_ref, g_ref, b_ref, y_ref, yb_ref, wub_ref, wdb_ref, acc_ref, *, alpha):
    def cast_weights():
        wu = wu_ref[...].astype(BF16)
        wd = wd_ref[...].astype(BF16)
        wub_ref[...] = wu
        wdb_ref[...] = wd
        return wu, wd

    _ffn_step(pl.program_id(0), pl.num_programs(0), xb_ref, x_ref, cast_weights, g_ref, b_ref, y_ref, yb_ref,
              acc_ref, alpha)


def _ffn_residual_ln_casting(xb, x, w_up, w_down, layer, g, b, alpha, tf=512):
    m, d = x.shape
    dff = w_up.shape[2]
    tf = _tile(dff, tf)
    return pl.pallas_call(
        functools.partial(_ffn_cast_kernel, alpha=alpha),
        grid=(dff // tf,),
        in_specs=[pl.BlockSpec((m, d), lambda f: (0, 0)),
                  pl.BlockSpec((m, d), lambda f: (0, 0)),
                  pl.BlockSpec((None, d, tf), lambda f: (layer, 0, f)),
                  pl.BlockSpec((None, tf, d), lambda f: (layer, f, 0)),
                  pl.BlockSpec((1, d), lambda f: (0, 0)),
                  pl.BlockSpec((1, d), lambda f: (0, 0))],
        out_specs=[pl.BlockSpec((m, d), lambda f: (0, 0)), pl.BlockSpec((m, d), lambda f: (0, 0)),
                   pl.BlockSpec((d, tf), lambda f: (0, f)), pl.BlockSpec((tf, d), lambda f: (f, 0))],
        out_shape=[jax.ShapeDtypeStruct((m, d), F32), jax.ShapeDtypeStruct((m, d), BF16),
                   jax.ShapeDtypeStruct((d, dff), BF16), jax.ShapeDtypeStruct((dff, d), BF16)],
        scratch_shapes=[pltpu.VMEM((m, d), F32)],
        compiler_params=_cparams(("arbitrary",)),
        name="ffn_ln_casting",
    )(xb, x, w_up, w_down, g, b)


def _pair_tables(n_blocks, descending):
    qi, kj = [], []
    for q in range(n_blocks):
        ks = range(q, -1, -1) if descending else range(q + 1)
        for k in ks:
            qi.append(q)
            kj.append(k)
    return jnp.asarray(qi, jnp.int32), jnp.asarray(kj, jnp.int32)


def _tri_iota(t):
    r = lax.broadcasted_iota(jnp.int32, (t, t), 0)
    c = lax.broadcasted_iota(jnp.int32, (t, t), 1)
    return r, c


def _causal(rows, width, row0=0, col0=0, strict=False):
    r = jnp.tile(lax.broadcasted_iota(jnp.int32, (rows, LANES), 0), (1, width // LANES)) + row0
    c = lax.broadcasted_iota(jnp.int32, (1, width), 1) + col0
    return (c < r) if strict else (c <= r)


def _row_strips(t, diagonal, rows=DIAG_STRIP):
    if not diagonal:
        return [(0, t, t)]
    rows = min(rows, t)
    return [(r0, r0 + rows, r0 + rows) for r0 in range(0, t, rows)]


def _lanes_to_rows(row, t):
    r, c = _tri_iota(t)
    col = jnp.sum(jnp.where(r == c, row, 0.0), axis=1, keepdims=True)
    return jnp.broadcast_to(col, (t, LANES))


def _softmax_step(a, row_shift, m_prev, l_prev, acc_prev, v):
    t, tk = a.shape
    m_next = jnp.maximum(m_prev, jnp.max(a, axis=1, keepdims=True) - row_shift)
    pr = jnp.exp2(a - jnp.tile(m_next + row_shift, (1, tk // LANES)))
    alpha = jnp.exp2(m_prev - m_next)
    l_next = alpha * l_prev + jnp.sum(pr, axis=1, keepdims=True)
    acc_next = jnp.tile(alpha, (1, v.shape[1] // LANES)) * acc_prev + _dot(pr.astype(BF16), v)
    return m_next, l_next, acc_next


def _fox_kernel(qi_tab, kj_tab, q_ref, k_ref, v_ref, ck_ref, cqrow_ref, o_ref, m_s, l_s, acc_s, cq_s, *, dh):
    p = pl.program_id(2)
    qi, kj = qi_tab[p], kj_tab[p]
    t = q_ref.shape[0]
    hb = m_s.shape[0]

    @pl.when(kj == 0)
    def _():
        m_s[...] = jnp.full(m_s.shape, -jnp.inf, F32)
        l_s[...] = jnp.zeros(l_s.shape, F32)
        acc_s[...] = jnp.zeros(acc_s.shape, F32)
        for i in range(hb):
            cq_s[i] = _lanes_to_rows(cqrow_ref[i], t)

    def step(diagonal):
        for i in range(hb):
            cols = slice(i * dh, (i + 1) * dh)
            for r0, r1, kc in _row_strips(t, diagonal):
                a = _dot_nt(q_ref[r0:r1, cols], k_ref[:kc, cols]) + ck_ref[i, :, :kc]
                if diagonal:
                    a = jnp.where(_causal(r1 - r0, kc, row0=r0), a, NEG)
                m, l, acc = _softmax_step(a, cq_s[i, r0:r1], m_s[i, r0:r1], l_s[i, r0:r1], acc_s[i, r0:r1],
                                          v_ref[:kc, cols])
                m_s[i, r0:r1] = m
                l_s[i, r0:r1] = l
                acc_s[i, r0:r1] = acc

    @pl.when(kj < qi)
    def _():
        step(False)

    @pl.when(kj == qi)
    def _():
        step(True)
        for i in range(hb):
            o_ref[:, i * dh:(i + 1) * dh] = (acc_s[i] / l_s[i]).astype(o_ref.dtype)


def _fox_attention(q, k, v, cum, batch, heads, dh, tb=512, hb=4):
    assert dh == LANES
    m = q.shape[0]
    t = m // batch
    tb = _tile(t, tb)
    hb = _tile(heads, hb)
    nb = t // tb
    qi_tab, kj_tab = _pair_tables(nb, descending=False)
    grid_spec = pltpu.PrefetchScalarGridSpec(
        num_scalar_prefetch=2,
        grid=(batch, heads // hb, int(qi_tab.shape[0])),
        in_specs=[pl.BlockSpec((tb, hb * dh), lambda b, h, p, qt, kt: (b * nb + qt[p], h)),
                  pl.BlockSpec((tb, hb * dh), lambda b, h, p, qt, kt: (b * nb + kt[p], h)),
                  pl.BlockSpec((tb, hb * dh), lambda b, h, p, qt, kt: (b * nb + kt[p], h)),
                  pl.BlockSpec((None, hb, 1, tb), lambda b, h, p, qt, kt: (b, h, 0, kt[p])),
                  pl.BlockSpec((None, hb, 1, tb), lambda b, h, p, qt, kt: (b, h, 0, qt[p]))],
        out_specs=pl.BlockSpec((tb, hb * dh), lambda b, h, p, qt, kt: (b * nb + qt[p], h)),
        scratch_shapes=[pltpu.VMEM((hb, tb, LANES), F32), pltpu.VMEM((hb, tb, LANES), F32),
                        pltpu.VMEM((hb, tb, dh), F32), pltpu.VMEM((hb, tb, LANES), F32)])
    return pl.pallas_call(
        functools.partial(_fox_kernel, dh=dh),
        grid_spec=grid_spec,
        out_shape=jax.ShapeDtypeStruct((m, heads * dh), BF16),
        compiler_params=_cparams(("parallel", "parallel", "arbitrary")),
        name="fox_attention",
    )(qi_tab, kj_tab, q, k, v, cum, cum)


def _sb_kernel(qi_tab, kj_tab, q_ref, k_ref, v_ref, o_ref, carry_s, acc_s, *, dh):
    p = pl.program_id(2)
    qi, kj = qi_tab[p], kj_tab[p]
    t = q_ref.shape[0]
    ch = min(CUM_CHUNK, t)
    hb = acc_s.shape[0]

    @pl.when(kj == qi)
    def _():
        carry_s[...] = jnp.zeros(carry_s.shape, F32)
        acc_s[...] = jnp.zeros(acc_s.shape, F32)

    def step(diagonal):
        u = _suffix_matrix(ch)
        for i in range(hb):
            cols = slice(i * dh, (i + 1) * dh)
            for r0, r1, kc in _row_strips(t, diagonal, ch):
                drop = carry_s[i, r0:r1]
                acc = acc_s[i, r0:r1]
                for c0 in reversed(range(0, kc, ch)):
                    z = _dot_nt(q_ref[r0:r1, cols], k_ref[c0:c0 + ch, cols])
                    sp = jnp.maximum(z, 0.0) + jnp.log2(1.0 + jnp.exp2(-jnp.abs(z)))
                    on_diagonal = diagonal and c0 + ch > r0
                    if on_diagonal:
                        strict = _causal(r1 - r0, ch, row0=r0, col0=c0, strict=True)
                        sp = jnp.where(strict, sp, 0.0)
                    hi, lo = _bf16_terms(sp, 2)
                    a = jnp.exp2(z - sp - (_dot(hi, u) + _dot(lo, u)) - jnp.tile(drop, (1, ch // LANES)))
                    if on_diagonal:
                        a = jnp.where(strict, a, 0.0)
                    acc = acc + _dot(a.astype(BF16), v_ref[c0:c0 + ch, cols])
                    drop = drop + jnp.sum(sp, axis=1, keepdims=True)
                carry_s[i, r0:r1] = drop
                acc_s[i, r0:r1] = acc

    @pl.when(kj == qi)
    def _():
        step(True)

    @pl.when(kj < qi)
    def _():
        step(False)

    @pl.when(kj == 0)
    def _():
        for i in range(hb):
            o_ref[:, i * dh:(i + 1) * dh] = acc_s[i].astype(o_ref.dtype)


def _sb_attention(q, k, v, batch, heads, dh, tb=512, hb=4):
    m = q.shape[0]
    t = m // batch
    tb = _tile(t, tb)
    hb = _tile(heads, hb)
    nb = t // tb
    qi_tab, kj_tab = _pair_tables(nb, descending=True)
    grid_spec = pltpu.PrefetchScalarGridSpec(
        num_scalar_prefetch=2,
        grid=(batch, heads // hb, int(qi_tab.shape[0])),
        in_specs=[pl.BlockSpec((tb, hb * dh), lambda b, h, p, qt, kt: (b * nb + qt[p], h)),
                  pl.BlockSpec((tb, hb * dh), lambda b, h, p, qt, kt: (b * nb + kt[p], h)),
                  pl.BlockSpec((tb, hb * dh), lambda b, h, p, qt, kt: (b * nb + kt[p], h))],
        out_specs=pl.BlockSpec((tb, hb * dh), lambda b, h, p, qt, kt: (b * nb + qt[p], h)),
        scratch_shapes=[pltpu.VMEM((hb, tb, LANES), F32), pltpu.VMEM((hb, tb, dh), F32)])
    return pl.pallas_call(
        functools.partial(_sb_kernel, dh=dh),
        grid_spec=grid_spec,
        out_shape=jax.ShapeDtypeStruct((m, heads * dh), BF16),
        compiler_params=_cparams(("parallel", "parallel", "arbitrary")),
        name="sb_attention",
    )(qi_tab, kj_tab, q, k, v)


def _lambda_value(lq1_ref, lk1_ref, lq2_ref, lk2_ref, lambda_init):
    a = jnp.sum(lq1_ref[...] * lk1_ref[...], axis=1, keepdims=True)
    b = jnp.sum(lq2_ref[...] * lk2_ref[...], axis=1, keepdims=True)
    return jnp.exp(a) - jnp.exp(b) + lambda_init


def _diff_kernel(qi_tab, kj_tab, slope_ref, q_ref, k_ref, v_ref, lq1_ref, lk1_ref, lq2_ref, lk2_ref,
                 g_ref, o_ref, m_s, l_s, acc_s, *, lambda_init, dq):
    hg = pl.program_id(1)
    p = pl.program_id(2)
    qi, kj = qi_tab[p], kj_tab[p]
    t = q_ref.shape[0]
    dv = 2 * dq
    hb = m_s.shape[0] // 2

    @pl.when(kj == 0)
    def _():
        m_s[...] = jnp.full(m_s.shape, -jnp.inf, F32)
        l_s[...] = jnp.zeros(l_s.shape, F32)
        acc_s[...] = jnp.zeros(acc_s.shape, F32)

    def step(diagonal):
        for j in range(hb):
            slope = slope_ref[hg * hb + j]
            col_bias = slope * lax.broadcasted_iota(jnp.int32, (1, t), 1).astype(F32)
            row_pos = (qi - kj) * t + lax.broadcasted_iota(jnp.int32, (t, LANES), 0)
            row_shift = slope * row_pos.astype(F32)
            for i in range(2):
                cols = slice(j * dv + i * dq, j * dv + (i + 1) * dq)
                s = 2 * j + i
                for r0, r1, kc in _row_strips(t, diagonal):
                    a = _dot_nt(q_ref[r0:r1, cols], k_ref[:kc, cols]) + col_bias[:, :kc]
                    if diagonal:
                        a = jnp.where(_causal(r1 - r0, kc, row0=r0), a, NEG)
                    m, l, acc = _softmax_step(a, row_shift[r0:r1], m_s[s, r0:r1], l_s[s, r0:r1], acc_s[s, r0:r1],
                                              v_ref[:kc, j * dv:(j + 1) * dv])
                    m_s[s, r0:r1] = m
                    l_s[s, r0:r1] = l
                    acc_s[s, r0:r1] = acc

    @pl.when(kj < qi)
    def _():
        step(False)

    @pl.when(kj == qi)
    def _():
        step(True)
        lam = _lambda_value(lq1_ref, lk1_ref, lq2_ref, lk2_ref, lambda_init)
        reps = dv // LANES
        for j in range(hb):
            o = (acc_s[2 * j] / jnp.tile(l_s[2 * j], (1, reps))
                 - lam * (acc_s[2 * j + 1] / jnp.tile(l_s[2 * j + 1], (1, reps))))
            o = o * lax.rsqrt(jnp.mean(o * o, axis=1, keepdims=True) + LN_EPS) * g_ref[...]
            o_ref[:, j * dv:(j + 1) * dv] = (o * (1.0 - lambda_init)).astype(o_ref.dtype)


def _diff_attention(q, k, v, slopes, lq1, lk1, lq2, lk2, g, batch, heads, dq, lambda_init, tb=512, hb=4):
    m = q.shape[0]
    t = m // batch
    tb = _tile(t, tb)
    hb = _tile(heads, hb)
    nb = t // tb
    dv = 2 * dq
    qi_tab, kj_tab = _pair_tables(nb, descending=False)
    vec = pl.BlockSpec((1, dq), lambda b, h, p, qt, kt, sl: (0, 0))
    grid_spec = pltpu.PrefetchScalarGridSpec(
        num_scalar_prefetch=3,
        grid=(batch, heads // hb, int(qi_tab.shape[0])),
        in_specs=[pl.BlockSpec((tb, hb * dv), lambda b, h, p, qt, kt, sl: (b * nb + qt[p], h)),
                  pl.BlockSpec((tb, hb * dv), lambda b, h, p, qt, kt, sl: (b * nb + kt[p], h)),
                  pl.BlockSpec((tb, hb * dv), lambda b, h, p, qt, kt, sl: (b * nb + kt[p], h)),
                  vec, vec, vec, vec,
                  pl.BlockSpec((1, dv), lambda b, h, p, qt, kt, sl: (0, 0))],
        out_specs=pl.BlockSpec((tb, hb * dv), lambda b, h, p, qt, kt, sl: (b * nb + qt[p], h)),
        scratch_shapes=[pltpu.VMEM((2 * hb, tb, LANES), F32), pltpu.VMEM((2 * hb, tb, LANES), F32),
                        pltpu.VMEM((2 * hb, tb, dv), F32)])
    return pl.pallas_call(
        functools.partial(_diff_kernel, lambda_init=lambda_init, dq=dq),
        grid_spec=grid_spec,
        out_shape=jax.ShapeDtypeStruct((m, heads * dv), BF16),
        compiler_params=_cparams(("parallel", "parallel", "arbitrary")),
        name="diff_attention",
    )(qi_tab, kj_tab, slopes, q, k, v, lq1, lk1, lq2, lk2, g)


def _own_head(rows, width, heads):
    assert heads & (heads - 1) == 0
    r = lax.broadcasted_iota(jnp.int32, (rows, width), 0)
    c = lax.broadcasted_iota(jnp.int32, (rows, width), 1)
    return (c & (heads - 1)) == (r & (heads - 1))


def _flat_pages(refs):
    return [ref[...].reshape(ref.shape[0] * ref.shape[1], ref.shape[2]).astype(BF16) for ref in refs]


def _wide_scores(q, k_refs):
    s = [_dot_nt(q, k) for k in _flat_pages(k_refs)]
    return jnp.concatenate(s, axis=1) if len(s) > 1 else s[0]


def _wide_pv(w, v_refs):
    out = None
    width = v_refs[0].shape[0] * v_refs[0].shape[1]
    for g, v in enumerate(_flat_pages(v_refs)):
        d = _dot(w[:, g * width:(g + 1) * width], v)
        out = d if out is None else out + d
    return out


def _wide_suffix_sums(x, carry, terms):
    rows, w = x.shape
    ch = min(CUM_CHUNK, w)
    n = w // ch
    u = _suffix_matrix(ch)
    parts = [p.astype(F32) for p in _bf16_terms(x, terms)]
    stacked = jnp.concatenate([p[:, c * ch:(c + 1) * ch] for c in range(n) for p in parts], axis=0)
    d = _dot(stacked.astype(BF16), u)
    outs = [None] * n
    for c in reversed(range(n)):
        base = c * terms * rows
        tail = d[base:base + rows]
        for t in range(1, terms):
            tail = tail + d[base + t * rows:base + (t + 1) * rows]
        outs[c] = tail + carry
        carry = carry + jnp.sum(x[:, c * ch:(c + 1) * ch], axis=1, keepdims=True)
    return (jnp.concatenate(outs, axis=1) if n > 1 else outs[0]), carry


def _fox_dec_kernel(pt_ref, q_ref, kn_ref, vn_ref, lfn_ref, *refs, g_pages, scale):
    lf_refs = refs[:g_pages]
    k_refs = refs[g_pages:2 * g_pages]
    v_refs = refs[2 * g_pages:3 * g_pages]
    o_ref, m_s, l_s, acc_s, carry_s = refs[3 * g_pages:]
    j = pl.program_id(1)
    heads = q_ref.shape[0]
    q = (q_ref[...] * scale).astype(BF16)

    @pl.when(j == 0)
    def _():
        qn = q.astype(F32) * kn_ref[...].astype(BF16).astype(F32)
        m_s[...] = jnp.sum(qn, axis=1, keepdims=True)
        l_s[...] = jnp.ones(l_s.shape, F32)
        acc_s[...] = vn_ref[...].astype(BF16).astype(F32)
        carry_s[...] = lfn_ref[...]

    s = _wide_scores(q, k_refs)
    own = _own_head(heads, s.shape[1], heads)
    lf = jnp.concatenate([r[...] for r in lf_refs], axis=1) if g_pages > 1 else lf_refs[0][...]
    cum, carry = _wide_suffix_sums(jnp.where(own, lf, 0.0), carry_s[...], 3)
    carry_s[...] = carry
    s = jnp.where(own, s + cum, NEG)
    m_old = m_s[...]
    m_new = jnp.maximum(m_old, jnp.max(s, axis=1, keepdims=True))
    alpha = jnp.exp(m_old - m_new)
    pr = jnp.exp(s - m_new)
    l_s[...] = alpha * l_s[...] + jnp.sum(pr, axis=1, keepdims=True)
    acc_s[...] = alpha * acc_s[...] + _wide_pv(pr.astype(BF16), v_refs)
    m_s[...] = m_new

    @pl.when(j == pl.num_programs(1) - 1)
    def _():
        o_ref[...] = (acc_s[...] / l_s[...]).astype(o_ref.dtype)


def _sb_dec_kernel(pt_ref, q_ref, *refs, g_pages, scale):
    k_refs = refs[:g_pages]
    v_refs = refs[g_pages:2 * g_pages]
    o_ref, acc_s, carry_s = refs[2 * g_pages:]
    j = pl.program_id(1)
    heads = q_ref.shape[0]
    q = (q_ref[...] * scale).astype(BF16)

    @pl.when(j == 0)
    def _():
        acc_s[...] = jnp.zeros(acc_s.shape, F32)
        carry_s[...] = jnp.zeros(carry_s.shape, F32)

    z = _wide_scores(q, k_refs)
    own = _own_head(heads, z.shape[1], heads)
    log_keep = jnp.where(own, _log_sigmoid(-z), 0.0)
    tail, carry = _wide_suffix_sums(log_keep, carry_s[...], 2)
    carry_s[...] = carry
    a = jnp.where(own, jnp.exp(log_keep + z + tail), 0.0)
    acc_s[...] += _wide_pv(a.astype(BF16), v_refs)

    @pl.when(j == pl.num_programs(1) - 1)
    def _():
        o_ref[...] = acc_s[...].astype(o_ref.dtype)


def _diff_dec_kernel(pt_ref, q_ref, kn_ref, vn_ref, slope_ref, lq1_ref, lk1_ref, lq2_ref, lk2_ref,
                     g_ref, *refs, g_pages, scale, lambda_init, n_pages):
    k_refs = refs[:g_pages]
    v_refs = refs[g_pages:2 * g_pages]
    o_ref, m_s, l_s, acc_s = refs[2 * g_pages:]
    j = pl.program_id(1)
    page, heads, _ = k_refs[0].shape
    rows = 2 * heads
    q = (q_ref[...] * scale).astype(BF16)

    @pl.when(j == 0)
    def _():
        kn = kn_ref[...].astype(BF16).astype(F32)
        qn = q.astype(F32) * jnp.concatenate([kn, kn], axis=0)
        m_s[...] = jnp.sum(qn, axis=1, keepdims=True)
        l_s[...] = jnp.ones(l_s.shape, F32)
        vn = vn_ref[...].astype(BF16).astype(F32)
        acc_s[...] = jnp.concatenate([vn, vn], axis=0)

    s = _wide_scores(q, k_refs)
    own = _own_head(rows, s.shape[1], heads)
    first = (n_pages - (j + 1) * g_pages) * page
    col = lax.broadcasted_iota(jnp.int32, s.shape, 1)
    dist = (n_pages * page - first - jnp.right_shift(col, heads.bit_length() - 1)).astype(F32)
    s = jnp.where(own, s - slope_ref[...] * dist, NEG)
    m_old = m_s[...]
    m_new = jnp.maximum(m_old, jnp.max(s, axis=1, keepdims=True))
    alpha = jnp.exp(m_old - m_new)
    pr = jnp.exp(s - m_new)
    l_s[...] = alpha * l_s[...] + jnp.sum(pr, axis=1, keepdims=True)
    acc_s[...] = alpha * acc_s[...] + _wide_pv(pr.astype(BF16), v_refs)
    m_s[...] = m_new

    @pl.when(j == pl.num_programs(1) - 1)
    def _():
        lam = _lambda_value(lq1_ref, lk1_ref, lq2_ref, lk2_ref, lambda_init)
        on = acc_s[...] / l_s[...]
        o = on[:heads] - lam * on[heads:]
        o = o * lax.rsqrt(jnp.mean(o * o, axis=1, keepdims=True) + LN_EPS) * g_ref[...]
        o_ref[...] = (o * (1.0 - lambda_init)).astype(o_ref.dtype)


def _page_specs(n_pages, g_pages, block, layer):
    def make(g):
        def index(b, j, pt):
            return (layer, pt[b, n_pages - (j + 1) * g_pages + g]) + (0,) * len(block)
        return pl.BlockSpec((None, None) + block, index)
    return [make(g) for g in range(g_pages)]


def _per_seq_spec(shape):
    return pl.BlockSpec((None,) + shape, lambda b, j, pt: (b,) + (0,) * len(shape))


def _shared_spec(shape):
    return pl.BlockSpec(shape, lambda b, j, pt: (0,) * len(shape))


def _fox_decode(page_table, q, kn, vn, lfn, lf_cache, k_cache, v_cache, layer, g_pages=16):
    bsz, n_pages = page_table.shape
    _, _, page, heads, dh = k_cache.shape
    g_pages = _tile(n_pages, g_pages)
    grid_spec = pltpu.PrefetchScalarGridSpec(
        num_scalar_prefetch=1,
        grid=(bsz, n_pages // g_pages),
        in_specs=([_per_seq_spec((heads, dh))] * 3 + [_per_seq_spec((heads, 1))]
                  + _page_specs(n_pages, g_pages, (1, page * heads), layer)
                  + _page_specs(n_pages, g_pages, (page, heads, dh), layer)
                  + _page_specs(n_pages, g_pages, (page, heads, dh), layer)),
        out_specs=_per_seq_spec((heads, dh)),
        scratch_shapes=[pltpu.VMEM((heads, 1), F32), pltpu.VMEM((heads, 1), F32),
                        pltpu.VMEM((heads, dh), F32), pltpu.VMEM((heads, 1), F32)])
    return pl.pallas_call(
        functools.partial(_fox_dec_kernel, g_pages=g_pages, scale=dh ** -0.5),
        grid_spec=grid_spec,
        out_shape=jax.ShapeDtypeStruct((bsz, heads, dh), BF16),
        compiler_params=_cparams(("parallel", "arbitrary")),
        name="fox_decode",
    )(page_table, q, kn, vn, lfn, *([lf_cache] * g_pages), *([k_cache] * g_pages), *([v_cache] * g_pages))


def _sb_decode(page_table, q, k_cache, v_cache, layer, g_pages=16):
    bsz, n_pages = page_table.shape
    _, _, page, heads, dh = k_cache.shape
    g_pages = _tile(n_pages, g_pages)
    grid_spec = pltpu.PrefetchScalarGridSpec(
        num_scalar_prefetch=1,
        grid=(bsz, n_pages // g_pages),
        in_specs=([_per_seq_spec((heads, dh))]
                  + _page_specs(n_pages, g_pages, (page, heads, dh), layer)
                  + _page_specs(n_pages, g_pages, (page, heads, dh), layer)),
        out_specs=_per_seq_spec((heads, dh)),
        scratch_shapes=[pltpu.VMEM((heads, dh), F32), pltpu.VMEM((heads, 1), F32)])
    return pl.pallas_call(
        functools.partial(_sb_dec_kernel, g_pages=g_pages, scale=dh ** -0.5),
        grid_spec=grid_spec,
        out_shape=jax.ShapeDtypeStruct((bsz, heads, dh), BF16),
        compiler_params=_cparams(("parallel", "arbitrary")),
        name="sb_decode",
    )(page_table, q, *([k_cache] * g_pages), *([v_cache] * g_pages))


def _diff_decode(page_table, q, kn, vn, slopes, lq1, lk1, lq2, lk2, g, k_cache, v_cache, layer, lambda_init,
                 g_pages=8):
    bsz, n_pages = page_table.shape
    _, _, page, heads, dv = k_cache.shape
    dq = dv // 2
    rows = 2 * heads
    g_pages = _tile(n_pages, g_pages)
    grid_spec = pltpu.PrefetchScalarGridSpec(
        num_scalar_prefetch=1,
        grid=(bsz, n_pages // g_pages),
        in_specs=([_per_seq_spec((rows, dv)), _per_seq_spec((heads, dv)), _per_seq_spec((heads, dv)),
                   _shared_spec((rows, 1)),
                   _shared_spec((1, dq)), _shared_spec((1, dq)), _shared_spec((1, dq)), _shared_spec((1, dq)),
                   _shared_spec((1, dv))]
                  + _page_specs(n_pages, g_pages, (page, heads, dv), layer)
                  + _page_specs(n_pages, g_pages, (page, heads, dv), layer)),
        out_specs=_per_seq_spec((heads, dv)),
        scratch_shapes=[pltpu.VMEM((rows, 1), F32), pltpu.VMEM((rows, 1), F32),
                        pltpu.VMEM((rows, dv), F32)])
    return pl.pallas_call(
        functools.partial(_diff_dec_kernel, g_pages=g_pages, scale=dq ** -0.5, lambda_init=lambda_init,
                          n_pages=n_pages),
        grid_spec=grid_spec,
        out_shape=jax.ShapeDtypeStruct((bsz, heads, dv), BF16),
        compiler_params=_cparams(("parallel", "arbitrary")),
        name="diff_decode",
    )(page_table, q, kn, vn, slopes, lq1, lk1, lq2, lk2, g, *([k_cache] * g_pages), *([v_cache] * g_pages))


def _even_layer(xpb, xsb, caches, page_table, w_in, b_f, w_o, dims):
    cache_k_a, cache_v_a, cache_lf, cache_k_b, cache_v_b, layer_idx = caches
    bsz, t, dbs, heads, dh = dims
    w = heads * dh
    scale = dh ** -0.5
    wb = w_in.astype(BF16)
    wb_b = wb[:, 3 * w + heads:]
    w_f = jnp.pad(wb[:, 3 * w:3 * w + heads], ((0, 0), (0, LANES - heads)))
    b_pad = jnp.pad(b_f.astype(F32), (0, LANES - heads)).reshape(1, LANES)
    wo_a, wo_b = w_o[:w].astype(BF16), w_o[w:].astype(BF16)

    (q_a,) = _mm(xpb, wb, (BF16,), out_scale=scale * LOG2E, col0=0, n=w)
    k_a, k_ab = _mm(xpb, wb, (F32, BF16), col0=w, n=w)
    v_a, v_ab = _mm(xpb, wb, (F32, BF16), col0=2 * w, n=w)
    (q_b,) = _mm(xpb, wb_b, (BF16,), out_scale=scale * LOG2E, col0=0, n=w)
    k_b, k_bb = _mm(xpb, wb_b, (F32, BF16), col0=w, n=w)
    v_b, v_bb = _mm(xpb, wb_b, (F32, BF16), col0=2 * w, n=w)
    lf = _forget_gate(xpb, w_f, b_pad)[:, :heads]
    lf_t = jnp.swapaxes(lf.reshape(bsz, t, heads), 1, 2)
    cum = _suffix_sums(lf_t.reshape(bsz * heads, t), LOG2E).reshape(bsz, heads, 1, t)
    o_a = _fox_attention(q_a, k_ab, v_ab, cum, bsz, heads, dh)
    o_b = _sb_attention(q_b, k_bb, v_bb, bsz, heads, dh)
    new_p = (k_a.reshape(bsz, t, heads, dh), v_a.reshape(bsz, t, heads, dh), lf.reshape(bsz, t, heads),
             k_b.reshape(bsz, t, heads, dh), v_b.reshape(bsz, t, heads, dh))

    (qkv_a,) = _mm(xsb, wb, (F32,), col0=0, n=3 * w)
    (qkv_b,) = _mm(xsb, wb_b, (F32,), col0=0, n=3 * w)
    sq_a, sk_a, sv_a = (qkv_a[:, i * w:(i + 1) * w] for i in range(3))
    sq_b, sk_b, sv_b = (qkv_b[:, i * w:(i + 1) * w] for i in range(3))
    slf = _forget_gate(xsb, w_f, b_pad)[:, :heads]
    per_head = lambda a: a.reshape(dbs, heads, dh)
    n_l, n_pool, page, _ = cache_lf.shape
    lf_flat = cache_lf.reshape(n_l, n_pool, 1, page * heads)
    so_a = _fox_decode(page_table, per_head(sq_a), per_head(sk_a), per_head(sv_a), slf.reshape(dbs, heads, 1),
                       lf_flat, cache_k_a, cache_v_a, layer_idx)
    so_b = _sb_decode(page_table, per_head(sq_b), cache_k_b, cache_v_b, layer_idx)
    new_s = (sk_a.reshape(dbs, 1, heads, dh), sv_a.reshape(dbs, 1, heads, dh), slf.reshape(dbs, 1, heads),
             sk_b.reshape(dbs, 1, heads, dh), sv_b.reshape(dbs, 1, heads, dh))
    return (o_a, o_b), (so_a.reshape(dbs, w), so_b.reshape(dbs, w)), (wo_a, wo_b), new_p, new_s


def _odd_layer(xpb, xsb, caches, page_table, w_in, lq1, lk1, lq2, lk2, gain, w_o, lambda_init, dims):
    cache_k, cache_v, layer_idx = caches
    bsz, t, dbs, heads, dq = dims
    dv = 2 * dq
    w = heads * dv
    scale = dq ** -0.5
    wb = w_in.astype(BF16)
    slopes = 2.0 ** (-8.0 * jnp.arange(1, heads + 1, dtype=F32) / heads)
    vecs = [a.astype(F32).reshape(1, dq) for a in (lq1, lk1, lq2, lk2)]
    g_row = gain.astype(F32).reshape(1, dv)

    (q,) = _mm(xpb, wb, (BF16,), out_scale=scale * LOG2E, col0=0, n=w)
    k, kb = _mm(xpb, wb, (F32, BF16), col0=w, n=w)
    v, vb = _mm(xpb, wb, (F32, BF16), col0=2 * w, n=w)
    o = _diff_attention(q, kb, vb, slopes * LOG2E, *vecs, g_row, bsz, heads, dq, lambda_init)
    new_p = (k.reshape(bsz, t, heads, dv), v.reshape(bsz, t, heads, dv))

    (qkv,) = _mm(xsb, wb, (F32,), col0=0, n=3 * w)
    sq, sk, sv = (qkv[:, i * w:(i + 1) * w] for i in range(3))
    sq4 = sq.reshape(dbs, heads, 2, dq)
    zero = jnp.zeros((dbs, heads, dq), F32)
    q_rows = jnp.concatenate([jnp.concatenate([sq4[:, :, 0], zero], axis=2),
                              jnp.concatenate([zero, sq4[:, :, 1]], axis=2)], axis=1)
    slope_rows = jnp.concatenate([slopes, slopes]).reshape(2 * heads, 1)
    so = _diff_decode(page_table, q_rows, sk.reshape(dbs, heads, dv), sv.reshape(dbs, heads, dv), slope_rows,
                      *vecs, g_row, cache_k, cache_v, layer_idx, lambda_init)
    new_s = (sk.reshape(dbs, 1, heads, dv), sv.reshape(dbs, 1, heads, dv))
    return (o,), (so.reshape(dbs, w),), (w_o.astype(BF16),), new_p, new_s


def kernel(x_prompt, x_sample, cache_fox_k, cache_fox_v, cache_fox_logf, cache_sb_k, cache_sb_v, cache_diff_k, cache_diff_v, page_table, w_in_even, b_forget, w_out_even, w_in_odd, lambda_q1, lambda_k1, lambda_q2, lambda_k2, subln_gain, w_out_odd, ln_mix_gain, ln_mix_bias, ln_ffn_gain, ln_ffn_bias, w_ffn_up, w_ffn_down):
    bsz, t, d = x_prompt.shape
    dbs, ts, _ = x_sample.shape
    assert ts == 1, "the sample group carries one new token per sequence"
    depth = ln_mix_gain.shape[0]
    alpha = (2 * depth) ** 0.25
    h_fox, dh = cache_fox_k.shape[3], cache_fox_k.shape[4]
    h_diff, dq = cache_diff_k.shape[3], cache_diff_k.shape[4] // 2
    assert cache_sb_k.shape[3:] == (h_fox, dh), "both head groups of an even layer share one geometry"

    xp = x_prompt.reshape(bsz * t, d)
    xs = x_sample.reshape(dbs, d)
    xpb, xsb = xp.astype(BF16), xs.astype(BF16)
    even_p, even_s, odd_p, odd_s = [], [], [], []
    row = lambda a: a.astype(F32).reshape(1, d)
    for layer in range(depth):
        i = layer // 2
        if layer % 2 == 0:
            caches = (cache_fox_k, cache_fox_v, cache_fox_logf, cache_sb_k, cache_sb_v, i)
            o_p, o_s, w_o, new_p, new_s = _even_layer(xpb, xsb, caches, page_table, w_in_even[i],
                                                      b_forget[i], w_out_even[i], (bsz, t, dbs, h_fox, dh))
            even_p.append(new_p)
            even_s.append(new_s)
        else:
            lambda_init = 0.8 - 0.6 * math.exp(-0.3 * layer)
            caches = (cache_diff_k, cache_diff_v, i)
            o_p, o_s, w_o, new_p, new_s = _odd_layer(xpb, xsb, caches, page_table, w_in_odd[i], lambda_q1[i],
                                                     lambda_k1[i], lambda_q2[i], lambda_k2[i], subln_gain[i],
                                                     w_out_odd[i], lambda_init, (bsz, t, dbs, h_diff, dq))
            odd_p.append(new_p)
            odd_s.append(new_s)
        g1, b1 = row(ln_mix_gain[layer]), row(ln_mix_bias[layer])
        g2, b2 = row(ln_ffn_gain[layer]), row(ln_ffn_bias[layer])
        xp, xpb = _proj_residual_ln(list(o_p), list(w_o), xp, g1, b1, alpha)
        xs, xsb = _proj_residual_ln(list(o_s), list(w_o), xs, g1, b1, alpha)
        xs, xsb, w_up, w_down = _ffn_residual_ln_casting(xsb, xs, w_ffn_up, w_ffn_down, layer, g2, b2, alpha)
        xp, xpb = _ffn_residual_ln(xpb, xp, w_up, w_down, g2, b2, alpha)

    def stk(rows, j):
        return jnp.stack([r[j] for r in rows])

    return (xp.reshape(bsz, t, d), xs.reshape(dbs, 1, d),
            stk(even_p, 0), stk(even_p, 1), stk(even_p, 2), stk(even_p, 3), stk(even_p, 4), stk(odd_p, 0), stk(odd_p, 1),
            stk(even_s, 0), stk(even_s, 1), stk(even_s, 2), stk(even_s, 3), stk(even_s, 4), stk(odd_s, 0), stk(odd_s, 1))
```

```python
import functools
import math

import jax
import jax.numpy as jnp
from jax import lax
from jax.experimental import pallas as pl
from jax.experimental.pallas import tpu as pltpu

F32 = jnp.float32
BF16 = jnp.bfloat16
LN_EPS = 1e-5
NEG = -1e30
LOG2E = 1.4426950408889634
LANES = 128
CUM_CHUNK = 256
DIAG_STRIP = 256
VMEM_LIMIT = 56 * 1024 * 1024


def _cparams(sem):
    return pltpu.CompilerParams(dimension_semantics=sem, vmem_limit_bytes=VMEM_LIMIT)


def _tile(n, pref):
    t = min(n, pref)
    while n % t:
        t //= 2
    return t


def _dot(a, b):
    return jnp.dot(a, b, preferred_element_type=F32)


def _dot_nt(a, b):
    return lax.dot_general(a, b, (((1,), (1,)), ((), ())), preferred_element_type=F32)


def _bf16_terms(x, terms):
    out = []
    r = x
    for t in range(terms):
        h = r.astype(BF16)
        out.append(h)
        if t + 1 < terms:
            r = r - h.astype(F32)
    return out


def _split_dot(x, u, terms):
    out = None
    for h in _bf16_terms(x, terms):
        d = _dot(h, u)
        out = d if out is None else out + d
    return out


def _suffix_matrix(n):
    r = lax.broadcasted_iota(jnp.int32, (n, n), 0)
    c = lax.broadcasted_iota(jnp.int32, (n, n), 1)
    return jnp.where(r > c, 1.0, 0.0).astype(BF16)


def _log_sigmoid(x):
    return jnp.minimum(x, 0.0) - jnp.log1p(jnp.exp(-jnp.abs(x)))


def _layer_norm(x, g, b):
    mu = jnp.mean(x, axis=-1, keepdims=True)
    xc = x - mu
    var = jnp.mean(xc * xc, axis=-1, keepdims=True)
    return xc * lax.rsqrt(var + LN_EPS) * g + b


def _mm_kernel(x_ref, w_ref, *o_refs, out_scale):
    acc = _dot(x_ref[...], w_ref[...])
    if out_scale != 1.0:
        acc = acc * out_scale
    for o_ref in o_refs:
        o_ref[...] = acc.astype(o_ref.dtype)


def _mm(x, w, out_dtypes, out_scale=1.0, col0=0, n=None, tm=1024, tn=1024):
    m, k = x.shape
    n = w.shape[1] - col0 if n is None else n
    tm, tn = _tile(m, tm), _tile(n, tn)
    assert col0 % tn == 0, "the column window must start on a column-block boundary"
    j0 = col0 // tn
    outs = pl.pallas_call(
        functools.partial(_mm_kernel, out_scale=out_scale),
        grid=(n // tn, m // tm),
        in_specs=[pl.BlockSpec((tm, k), lambda j, i: (i, 0)),
                  pl.BlockSpec((k, tn), lambda j, i: (0, j0 + j))],
        out_specs=[pl.BlockSpec((tm, tn), lambda j, i: (i, j)) for _ in out_dtypes],
        out_shape=[jax.ShapeDtypeStruct((m, n), d) for d in out_dtypes],
        compiler_params=_cparams(("parallel", "parallel")),
        name="proj",
    )(x, w)
    return outs


def _entry_kernel(x_ref, w_ref, wf_ref, b_ref, q_ref, xb_ref, gate_ref, *, out_scale):
    xb = x_ref[...].astype(BF16)
    xb_ref[...] = xb
    q_ref[...] = (_dot(xb, w_ref[...]) * out_scale).astype(q_ref.dtype)
    gate_ref[...] = _log_sigmoid(_dot(xb, wf_ref[...]) + b_ref[...])


def _entry_projection(x, w, n, out_scale, w_f, b, tm=1024):
    m, k = x.shape
    tm = _tile(m, tm)
    nf = w_f.shape[1]
    return pl.pallas_call(
        functools.partial(_entry_kernel, out_scale=out_scale),
        grid=(m // tm,),
        in_specs=[pl.BlockSpec((tm, k), lambda i: (i, 0)),
                  pl.BlockSpec((k, n), lambda i: (0, 0)),
                  pl.BlockSpec((k, nf), lambda i: (0, 0)),
                  pl.BlockSpec((1, nf), lambda i: (0, 0))],
        out_specs=[pl.BlockSpec((tm, n), lambda i: (i, 0)), pl.BlockSpec((tm, k), lambda i: (i, 0)),
                   pl.BlockSpec((tm, nf), lambda i: (i, 0))],
        out_shape=[jax.ShapeDtypeStruct((m, n), BF16), jax.ShapeDtypeStruct((m, k), BF16),
                   jax.ShapeDtypeStruct((m, nf), F32)],
        compiler_params=_cparams(("parallel",)),
        name="entry_projection",
    )(x, w, w_f, b)


def _gate_kernel(x_ref, w_ref, b_ref, o_ref):
    o_ref[...] = _log_sigmoid(_dot(x_ref[...], w_ref[...]) + b_ref[...])


def _forget_gate(x, w, b, tm=512):
    m, k = x.shape
    n = w.shape[1]
    tm = _tile(m, tm)
    return pl.pallas_call(
        _gate_kernel,
        grid=(m // tm,),
        in_specs=[pl.BlockSpec((tm, k), lambda i: (i, 0)),
                  pl.BlockSpec((k, n), lambda i: (0, 0)),
                  pl.BlockSpec((1, n), lambda i: (0, 0))],
        out_specs=pl.BlockSpec((tm, n), lambda i: (i, 0)),
        out_shape=jax.ShapeDtypeStruct((m, n), F32),
        compiler_params=_cparams(("parallel",)),
        name="forget_gate",
    )(x, w, b)


def _revcum_kernel(x_ref, o_ref, *, out_scale):
    rows, t = x_ref.shape
    ch = min(CUM_CHUNK, t)
    u = _suffix_matrix(ch)
    carry = jnp.zeros((rows, 1), F32)
    for c in reversed(range(t // ch)):
        xc = x_ref[:, c * ch:(c + 1) * ch]
        o_ref[:, c * ch:(c + 1) * ch] = (_split_dot(xc, u, 3) + carry) * out_scale
        carry = carry + jnp.sum(xc, axis=1, keepdims=True)


def _suffix_sums(x, out_scale):
    return pl.pallas_call(
        functools.partial(_revcum_kernel, out_scale=out_scale),
        out_shape=jax.ShapeDtypeStruct(x.shape, F32),
        name="suffix_sums",
    )(x)


def _proj_ln_kernel(*refs, n_pairs, alpha):
    a_refs = refs[:n_pairs]
    w_refs = refs[n_pairs:2 * n_pairs]
    x_ref, g_ref, b_ref, y_ref, yb_ref = refs[2 * n_pairs:]
    acc = alpha * x_ref[...]
    for a_ref, w_ref in zip(a_refs, w_refs):
        acc = acc + _dot(a_ref[...], w_ref[...])
    y = _layer_norm(acc, g_ref[...], b_ref[...])
    y_ref[...] = y
    yb_ref[...] = y.astype(BF16)


def _proj_residual_ln(a_list, w_list, x, g, b, alpha, tm=512):
    m, d = x.shape
    tm = _tile(m, tm)
    n_pairs = len(a_list)
    in_specs = ([pl.BlockSpec((tm, a.shape[1]), lambda i: (i, 0)) for a in a_list]
                + [pl.BlockSpec(w.shape, lambda i: (0, 0)) for w in w_list]
                + [pl.BlockSpec((tm, d), lambda i: (i, 0)),
                   pl.BlockSpec((1, d), lambda i: (0, 0)),
                   pl.BlockSpec((1, d), lambda i: (0, 0))])
    return pl.pallas_call(
        functools.partial(_proj_ln_kernel, n_pairs=n_pairs, alpha=alpha),
        grid=(m // tm,),
        in_specs=in_specs,
        out_specs=[pl.BlockSpec((tm, d), lambda i: (i, 0)), pl.BlockSpec((tm, d), lambda i: (i, 0))],
        out_shape=[jax.ShapeDtypeStruct((m, d), F32), jax.ShapeDtypeStruct((m, d), BF16)],
        compiler_params=_cparams(("parallel",)),
        name="out_proj_ln",
    )(*a_list, *w_list, x, g, b)


def _ffn_step(f, n_f, xb_ref, x_ref, load_weights, g_ref, b_ref, y_ref, yb_ref, acc_ref, alpha):
    @pl.when(f == 0)
    def _():
        acc_ref[...] = alpha * x_ref[...]

    wu, wd = load_weights()
    h = jnp.maximum(_dot(xb_ref[...], wu), 0.0)
    acc_ref[...] += _dot((h * h).astype(BF16), wd)

    @pl.when(f == n_f - 1)
    def _():
        y = _layer_norm(acc_ref[...], g_ref[...], b_ref[...])
        y_ref[...] = y
        yb_ref[...] = y.astype(BF16)


def _ffn_kernel(xb_ref, x_ref, wu_ref, wd_ref, g_ref, b_ref, y_ref, yb_ref, acc_ref, *, alpha):
    _ffn_step(pl.program_id(1), pl.num_programs(1), xb_ref, x_ref, lambda: (wu_ref[...], wd_ref[...]), g_ref, b_ref,
              y_ref, yb_ref, acc_ref, alpha)


def _ffn_residual_ln(xb, x, w_up, w_down, g, b, alpha, tm=512, tf=1024):
    m, d = x.shape
    dff = w_up.shape[1]
    tm, tf = _tile(m, tm), _tile(dff, tf)
    return pl.pallas_call(
        functools.partial(_ffn_kernel, alpha=alpha),
        grid=(m // tm, dff // tf),
        in_specs=[pl.BlockSpec((tm, d), lambda i, f: (i, 0)),
                  pl.BlockSpec((tm, d), lambda i, f: (i, 0)),
                  pl.BlockSpec((d, tf), lambda i, f: (0, f)),
                  pl.BlockSpec((tf, d), lambda i, f: (f, 0)),
                  pl.BlockSpec((1, d), lambda i, f: (0, 0)),
                  pl.BlockSpec((1, d), lambda i, f: (0, 0))],
        out_specs=[pl.BlockSpec((tm, d), lambda i, f: (i, 0)), pl.BlockSpec((tm, d), lambda i, f: (i, 0))],
        out_shape=[jax.ShapeDtypeStruct((m, d), F32), jax.ShapeDtypeStruct((m, d), BF16)],
        scratch_shapes=[pltpu.VMEM((tm, d), F32)],
        compiler_params=_cparams(("parallel", "arbitrary")),
        name="ffn_ln",
    )(xb, x, w_up, w_down, g, b)


def _ffn_cast_kernel(xb_ref, x_ref, wu_ref, wd_ref, g_ref, b_ref, y_ref, yb_ref, wub_ref, wdb_ref, acc_ref, *, alpha):
    def cast_weights():
        wu = wu_ref[...].astype(BF16)
        wd = wd_ref[...].astype(BF16)
        wub_ref[...] = wu
        wdb_ref[...] = wd
        return wu, wd

    _ffn_step(pl.program_id(0), pl.num_programs(0), xb_ref, x_ref, cast_weights, g_ref, b_ref, y_ref, yb_ref,
              acc_ref, alpha)


def _ffn_residual_ln_casting(xb, x, w_up, w_down, layer, g, b, alpha, tf=512):
    m, d = x.shape
    dff = w_up.shape[2]
    tf = _tile(dff, tf)
    return pl.pallas_call(
        functools.partial(_ffn_cast_kernel, alpha=alpha),
        grid=(dff // tf,),
        in_specs=[pl.BlockSpec((m, d), lambda f: (0, 0)),
                  pl.BlockSpec((m, d), lambda f: (0, 0)),
                  pl.BlockSpec((None, d, tf), lambda f: (layer, 0, f)),
                  pl.BlockSpec((None, tf, d), lambda f: (layer, f, 0)),
                  pl.BlockSpec((1, d), lambda f: (0, 0)),
                  pl.BlockSpec((1, d), lambda f: (0, 0))],
        out_specs=[pl.BlockSpec((m, d), lambda f: (0, 0)), pl.BlockSpec((m, d), lambda f: (0, 0)),
                   pl.BlockSpec((d, tf), lambda f: (0, f)), pl.BlockSpec((tf, d), lambda f: (f, 0))],
        out_shape=[jax.ShapeDtypeStruct((m, d), F32), jax.ShapeDtypeStruct((m, d), BF16),
                   jax.ShapeDtypeStruct((d, dff), BF16), jax.ShapeDtypeStruct((dff, d), BF16)],
        scratch_shapes=[pltpu.VMEM((m, d), F32)],
        compiler_params=_cparams(("arbitrary",)),
        name="ffn_ln_casting",
    )(xb, x, w_up, w_down, g, b)


def _pair_tables(n_blocks, descending):
    qi, kj = [], []
    for q in range(n_blocks):
        ks = range(q, -1, -1) if descending else range(q + 1)
        for k in ks:
            qi.append(q)
            kj.append(k)
    return jnp.asarray(qi, jnp.int32), jnp.asarray(kj, jnp.int32)


def _tri_iota(t):
    r = lax.broadcasted_iota(jnp.int32, (t, t), 0)
    c = lax.broadcasted_iota(jnp.int32, (t, t), 1)
    return r, c


def _causal(rows, width, row0=0, col0=0, strict=False):
    r = jnp.tile(lax.broadcasted_iota(jnp.int32, (rows, LANES), 0), (1, width // LANES)) + row0
    c = lax.broadcasted_iota(jnp.int32, (1, width), 1) + col0
    return (c < r) if strict else (c <= r)


def _row_strips(t, diagonal, rows=DIAG_STRIP):
    if not diagonal:
        return [(0, t, t)]
    rows = min(rows, t)
    return [(r0, r0 + rows, r0 + rows) for r0 in range(0, t, rows)]


def _lanes_to_rows(row, t):
    r, c = _tri_iota(t)
    col = jnp.sum(jnp.where(r == c, row, 0.0), axis=1, keepdims=True)
    return jnp.broadcast_to(col, (t, LANES))


def _softmax_step(a, row_shift, m_prev, l_prev, acc_prev, v):
    t, tk = a.shape
    m_next = jnp.maximum(m_prev, jnp.max(a, axis=1, keepdims=True) - row_shift)
    pr = jnp.exp2(a - jnp.tile(m_next + row_shift, (1, tk // LANES)))
    alpha = jnp.exp2(m_prev - m_next)
    l_next = alpha * l_prev + jnp.sum(pr, axis=1, keepdims=True)
    acc_next = jnp.tile(alpha, (1, v.shape[1] // LANES)) * acc_prev + _dot(pr.astype(BF16), v)
    return m_next, l_next, acc_next


def _fox_kernel(qi_tab, kj_tab, q_ref, k_ref, v_ref, ck_ref, cqrow_ref, o_ref, m_s, l_s, acc_s, cq_s, *, dh):
    p = pl.program_id(2)
    qi, kj = qi_tab[p], kj_tab[p]
    t = q_ref.shape[0]
    hb = m_s.shape[0]

    @pl.when(kj == 0)
    def _():
        m_s[...] = jnp.full(m_s.shape, -jnp.inf, F32)
        l_s[...] = jnp.zeros(l_s.shape, F32)
        acc_s[...] = jnp.zeros(acc_s.shape, F32)
        for i in range(hb):
            cq_s[i] = _lanes_to_rows(cqrow_ref[i], t)

    def step(diagonal):
        for i in range(hb):
            cols = slice(i * dh, (i + 1) * dh)
            for r0, r1, kc in _row_strips(t, diagonal):
                a = _dot_nt(q_ref[r0:r1, cols], k_ref[:kc, cols]) + ck_ref[i, :, :kc]
                if diagonal:
                    a = jnp.where(_causal(r1 - r0, kc, row0=r0), a, NEG)
                m, l, acc = _softmax_step(a, cq_s[i, r0:r1], m_s[i, r0:r1], l_s[i, r0:r1], acc_s[i, r0:r1],
                                          v_ref[:kc, cols])
                m_s[i, r0:r1] = m
                l_s[i, r0:r1] = l
                acc_s[i, r0:r1] = acc

    @pl.when(kj < qi)
    def _():
        step(False)

    @pl.when(kj == qi)
    def _():
        step(True)
        for i in range(hb):
            o_ref[:, i * dh:(i + 1) * dh] = (acc_s[i] / l_s[i]).astype(o_ref.dtype)


def _fox_attention(q, k, v, cum, batch, heads, dh, tb=512, hb=4):
    assert dh == LANES
    m = q.shape[0]
    t = m // batch
    tb = _tile(t, tb)
    hb = _tile(heads, hb)
    nb = t // tb
    qi_tab, kj_tab = _pair_tables(nb, descending=False)
    grid_spec = pltpu.PrefetchScalarGridSpec(
        num_scalar_prefetch=2,
        grid=(batch, heads // hb, int(qi_tab.shape[0])),
        in_specs=[pl.BlockSpec((tb, hb * dh), lambda b, h, p, qt, kt: (b * nb + qt[p], h)),
                  pl.BlockSpec((tb, hb * dh), lambda b, h, p, qt, kt: (b * nb + kt[p], h)),
                  pl.BlockSpec((tb, hb * dh), lambda b, h, p, qt, kt: (b * nb + kt[p], h)),
                  pl.BlockSpec((None, hb, 1, tb), lambda b, h, p, qt, kt: (b, h, 0, kt[p])),
                  pl.BlockSpec((None, hb, 1, tb), lambda b, h, p, qt, kt: (b, h, 0, qt[p]))],
        out_specs=pl.BlockSpec((tb, hb * dh), lambda b, h, p, qt, kt: (b * nb + qt[p], h)),
        scratch_shapes=[pltpu.VMEM((hb, tb, LANES), F32), pltpu.VMEM((hb, tb, LANES), F32),
                        pltpu.VMEM((hb, tb, dh), F32), pltpu.VMEM((hb, tb, LANES), F32)])
    return pl.pallas_call(
        functools.partial(_fox_kernel, dh=dh),
        grid_spec=grid_spec,
        out_shape=jax.ShapeDtypeStruct((m, heads * dh), BF16),
        compiler_params=_cparams(("parallel", "parallel", "arbitrary")),
        name="fox_attention",
    )(qi_tab, kj_tab, q, k, v, cum, cum)


def _sb_kernel(qi_tab, kj_tab, q_ref, k_ref, v_ref, o_ref, carry_s, acc_s, *, dh):
    p = pl.program_id(2)
    qi, kj = qi_tab[p], kj_tab[p]
    t = q_ref.shape[0]
    ch = min(CUM_CHUNK, t)
    hb = acc_s.shape[0]

    @pl.when(kj == qi)
    def _():
        carry_s[...] = jnp.zeros(carry_s.shape, F32)
        acc_s[...] = jnp.zeros(acc_s.shape, F32)

    def step(diagonal):
        u = _suffix_matrix(ch)
        for i in range(hb):
            cols = slice(i * dh, (i + 1) * dh)
            for r0, r1, kc in _row_strips(t, diagonal, ch):
                drop = carry_s[i, r0:r1]
                acc = acc_s[i, r0:r1]
                for c0 in reversed(range(0, kc, ch)):
                    z = _dot_nt(q_ref[r0:r1, cols], k_ref[c0:c0 + ch, cols])
                    sp = jnp.maximum(z, 0.0) + jnp.log2(1.0 + jnp.exp2(-jnp.abs(z)))
                    on_diagonal = diagonal and c0 + ch > r0
                    if on_diagonal:
                        strict = _causal(r1 - r0, ch, row0=r0, col0=c0, strict=True)
                        sp = jnp.where(strict, sp, 0.0)
                    hi, lo = _bf16_terms(sp, 2)
                    a = jnp.exp2(z - sp - (_dot(hi, u) + _dot(lo, u)) - jnp.tile(drop, (1, ch // LANES)))
                    if on_diagonal:
                        a = jnp.where(strict, a, 0.0)
                    acc = acc + _dot(a.astype(BF16), v_ref[c0:c0 + ch, cols])
                    drop = drop + jnp.sum(sp, axis=1, keepdims=True)
                carry_s[i, r0:r1] = drop
                acc_s[i, r0:r1] = acc

    @pl.when(kj == qi)
    def _():
        step(True)

    @pl.when(kj < qi)
    def _():
        step(False)

    @pl.when(kj == 0)
    def _():
        for i in range(hb):
            o_ref[:, i * dh:(i + 1) * dh] = acc_s[i].astype(o_ref.dtype)


def _sb_attention(q, k, v, batch, heads, dh, tb=512, hb=4):
    m = q.shape[0]
    t = m // batch
    tb = _tile(t, tb)
    hb = _tile(heads, hb)
    nb = t // tb
    qi_tab, kj_tab = _pair_tables(nb, descending=True)
    grid_spec = pltpu.PrefetchScalarGridSpec(
        num_scalar_prefetch=2,
        grid=(batch, heads // hb, int(qi_tab.shape[0])),
        in_specs=[pl.BlockSpec((tb, hb * dh), lambda b, h, p, qt, kt: (b * nb + qt[p], h)),
                  pl.BlockSpec((tb, hb * dh), lambda b, h, p, qt, kt: (b * nb + kt[p], h)),
                  pl.BlockSpec((tb, hb * dh), lambda b, h, p, qt, kt: (b * nb + kt[p], h))],
        out_specs=pl.BlockSpec((tb, hb * dh), lambda b, h, p, qt, kt: (b * nb + qt[p], h)),
        scratch_shapes=[pltpu.VMEM((hb, tb, LANES), F32), pltpu.VMEM((hb, tb, dh), F32)])
    return pl.pallas_call(
        functools.partial(_sb_kernel, dh=dh),
        grid_spec=grid_spec,
        out_shape=jax.ShapeDtypeStruct((m, heads * dh), BF16),
        compiler_params=_cparams(("parallel", "parallel", "arbitrary")),
        name="sb_attention",
    )(qi_tab, kj_tab, q, k, v)


def _lambda_value(lq1_ref, lk1_ref, lq2_ref, lk2_ref, lambda_init):
    a = jnp.sum(lq1_ref[...] * lk1_ref[...], axis=1, keepdims=True)
    b = jnp.sum(lq2_ref[...] * lk2_ref[...], axis=1, keepdims=True)
    return jnp.exp(a) - jnp.exp(b) + lambda_init


def _diff_kernel(qi_tab, kj_tab, slope_ref, q_ref, k_ref, v_ref, lq1_ref, lk1_ref, lq2_ref, lk2_ref,
                 g_ref, o_ref, m_s, l_s, acc_s, *, lambda_init, dq):
    hg = pl.program_id(1)
    p = pl.program_id(2)
    qi, kj = qi_tab[p], kj_tab[p]
    t = q_ref.shape[0]
    dv = 2 * dq
    hb = m_s.shape[0] // 2

    @pl.when(kj == 0)
    def _():
        m_s[...] = jnp.full(m_s.shape, -jnp.inf, F32)
        l_s[...] = jnp.zeros(l_s.shape, F32)
        acc_s[...] = jnp.zeros(acc_s.shape, F32)

    def step(diagonal):
        for j in range(hb):
            slope = slope_ref[hg * hb + j]
            col_bias = slope * lax.broadcasted_iota(jnp.int32, (1, t), 1).astype(F32)
            row_pos = (qi - kj) * t + lax.broadcasted_iota(jnp.int32, (t, LANES), 0)
            row_shift = slope * row_pos.astype(F32)
            for i in range(2):
                cols = slice(j * dv + i * dq, j * dv + (i + 1) * dq)
                s = 2 * j + i
                for r0, r1, kc in _row_strips(t, diagonal):
                    a = _dot_nt(q_ref[r0:r1, cols], k_ref[:kc, cols]) + col_bias[:, :kc]
                    if diagonal:
                        a = jnp.where(_causal(r1 - r0, kc, row0=r0), a, NEG)
                    m, l, acc = _softmax_step(a, row_shift[r0:r1], m_s[s, r0:r1], l_s[s, r0:r1], acc_s[s, r0:r1],
                                              v_ref[:kc, j * dv:(j + 1) * dv])
                    m_s[s, r0:r1] = m
                    l_s[s, r0:r1] = l
                    acc_s[s, r0:r1] = acc

    @pl.when(kj < qi)
    def _():
        step(False)

    @pl.when(kj == qi)
    def _():
        step(True)
        lam = _lambda_value(lq1_ref, lk1_ref, lq2_ref, lk2_ref, lambda_init)
        reps = dv // LANES
        for j in range(hb):
            o = (acc_s[2 * j] / jnp.tile(l_s[2 * j], (1, reps))
                 - lam * (acc_s[2 * j + 1] / jnp.tile(l_s[2 * j + 1], (1, reps))))
            o = o * lax.rsqrt(jnp.mean(o * o, axis=1, keepdims=True) + LN_EPS) * g_ref[...]
            o_ref[:, j * dv:(j + 1) * dv] = (o * (1.0 - lambda_init)).astype(o_ref.dtype)


def _diff_attention(q, k, v, slopes, lq1, lk1, lq2, lk2, g, batch, heads, dq, lambda_init, tb=512, hb=4):
    m = q.shape[0]
    t = m // batch
    tb = _tile(t, tb)
    hb = _tile(heads, hb)
    nb = t // tb
    dv = 2 * dq
    qi_tab, kj_tab = _pair_tables(nb, descending=False)
    vec = pl.BlockSpec((1, dq), lambda b, h, p, qt, kt, sl: (0, 0))
    grid_spec = pltpu.PrefetchScalarGridSpec(
        num_scalar_prefetch=3,
        grid=(batch, heads // hb, int(qi_tab.shape[0])),
        in_specs=[pl.BlockSpec((tb, hb * dv), lambda b, h, p, qt, kt, sl: (b * nb + qt[p], h)),
                  pl.BlockSpec((tb, hb * dv), lambda b, h, p, qt, kt, sl: (b * nb + kt[p], h)),
                  pl.BlockSpec((tb, hb * dv), lambda b, h, p, qt, kt, sl: (b * nb + kt[p], h)),
                  vec, vec, vec, vec,
                  pl.BlockSpec((1, dv), lambda b, h, p, qt, kt, sl: (0, 0))],
        out_specs=pl.BlockSpec((tb, hb * dv), lambda b, h, p, qt, kt, sl: (b * nb + qt[p], h)),
        scratch_shapes=[pltpu.VMEM((2 * hb, tb, LANES), F32), pltpu.VMEM((2 * hb, tb, LANES), F32),
                        pltpu.VMEM((2 * hb, tb, dv), F32)])
    return pl.pallas_call(
        functools.partial(_diff_kernel, lambda_init=lambda_init, dq=dq),
        grid_spec=grid_spec,
        out_shape=jax.ShapeDtypeStruct((m, heads * dv), BF16),
        compiler_params=_cparams(("parallel", "parallel", "arbitrary")),
        name="diff_attention",
    )(qi_tab, kj_tab, slopes, q, k, v, lq1, lk1, lq2, lk2, g)


def _own_head(rows, width, heads):
    assert heads & (heads - 1) == 0
    r = lax.broadcasted_iota(jnp.int32, (rows, width), 0)
    c = lax.broadcasted_iota(jnp.int32, (rows, width), 1)
    return (c & (heads - 1)) == (r & (heads - 1))


def _flat_pages(refs):
    return [ref[...].reshape(ref.shape[0] * ref.shape[1], ref.shape[2]).astype(BF16) for ref in refs]


def _wide_scores(q, k_refs):
    s = [_dot_nt(q, k) for k in _flat_pages(k_refs)]
    return jnp.concatenate(s, axis=1) if len(s) > 1 else s[0]


def _wide_pv(w, v_refs):
    out = None
    width = v_refs[0].shape[0] * v_refs[0].shape[1]
    for g, v in enumerate(_flat_pages(v_refs)):
        d = _dot(w[:, g * width:(g + 1) * width], v)
        out = d if out is None else out + d
    return out


def _wide_suffix_sums(x, carry, terms):
    rows, w = x.shape
    ch = min(CUM_CHUNK, w)
    n = w // ch
    u = _suffix_matrix(ch)
    parts = [p.astype(F32) for p in _bf16_terms(x, terms)]
    stacked = jnp.concatenate([p[:, c * ch:(c + 1) * ch] for c in range(n) for p in parts], axis=0)
    d = _dot(stacked.astype(BF16), u)
    outs = [None] * n
    for c in reversed(range(n)):
        base = c * terms * rows
        tail = d[base:base + rows]
        for t in range(1, terms):
            tail = tail + d[base + t * rows:base + (t + 1) * rows]
        outs[c] = tail + carry
        carry = carry + jnp.sum(x[:, c * ch:(c + 1) * ch], axis=1, keepdims=True)
    return (jnp.concatenate(outs, axis=1) if n > 1 else outs[0]), carry


def _fox_dec_kernel(pt_ref, q_ref, kn_ref, vn_ref, lfn_ref, *refs, g_pages, scale):
    lf_refs = refs[:g_pages]
    k_refs = refs[g_pages:2 * g_pages]
    v_refs = refs[2 * g_pages:3 * g_pages]
    o_ref, m_s, l_s, acc_s, carry_s = refs[3 * g_pages:]
    j = pl.program_id(1)
    heads = q_ref.shape[0]
    q = (q_ref[...] * scale).astype(BF16)

    @pl.when(j == 0)
    def _():
        qn = q.astype(F32) * kn_ref[...].astype(BF16).astype(F32)
        m_s[...] = jnp.sum(qn, axis=1, keepdims=True)
        l_s[...] = jnp.ones(l_s.shape, F32)
        acc_s[...] = vn_ref[...].astype(BF16).astype(F32)
        carry_s[...] = lfn_ref[...]

    s = _wide_scores(q, k_refs)
    own = _own_head(heads, s.shape[1], heads)
    lf = jnp.concatenate([r[...] for r in lf_refs], axis=1) if g_pages > 1 else lf_refs[0][...]
    cum, carry = _wide_suffix_sums(jnp.where(own, lf, 0.0), carry_s[...], 3)
    carry_s[...] = carry
    s = jnp.where(own, s + cum, NEG)
    m_old = m_s[...]
    m_new = jnp.maximum(m_old, jnp.max(s, axis=1, keepdims=True))
    alpha = jnp.exp(m_old - m_new)
    pr = jnp.exp(s - m_new)
    l_s[...] = alpha * l_s[...] + jnp.sum(pr, axis=1, keepdims=True)
    acc_s[...] = alpha * acc_s[...] + _wide_pv(pr.astype(BF16), v_refs)
    m_s[...] = m_new

    @pl.when(j == pl.num_programs(1) - 1)
    def _():
        o_ref[...] = (acc_s[...] / l_s[...]).astype(o_ref.dtype)


def _sb_dec_kernel(pt_ref, q_ref, *refs, g_pages, scale):
    k_refs = refs[:g_pages]
    v_refs = refs[g_pages:2 * g_pages]
    o_ref, acc_s, carry_s = refs[2 * g_pages:]
    j = pl.program_id(1)
    heads = q_ref.shape[0]
    q = (q_ref[...] * scale).astype(BF16)

    @pl.when(j == 0)
    def _():
        acc_s[...] = jnp.zeros(acc_s.shape, F32)
        carry_s[...] = jnp.zeros(carry_s.shape, F32)

    z = _wide_scores(q, k_refs)
    own = _own_head(heads, z.shape[1], heads)
    log_keep = jnp.where(own, _log_sigmoid(-z), 0.0)
    tail, carry = _wide_suffix_sums(log_keep, carry_s[...], 2)
    carry_s[...] = carry
    a = jnp.where(own, jnp.exp(log_keep + z + tail), 0.0)
    acc_s[...] += _wide_pv(a.astype(BF16), v_refs)

    @pl.when(j == pl.num_programs(1) - 1)
    def _():
        o_ref[...] = acc_s[...].astype(o_ref.dtype)


def _diff_dec_kernel(pt_ref, q_ref, kn_ref, vn_ref, slope_ref, lq1_ref, lk1_ref, lq2_ref, lk2_ref,
                     g_ref, *refs, g_pages, scale, lambda_init, n_pages):
    k_refs = refs[:g_pages]
    v_refs = refs[g_pages:2 * g_pages]
    o_ref, m_s, l_s, acc_s = refs[2 * g_pages:]
    j = pl.program_id(1)
    page, heads, _ = k_refs[0].shape
    rows = 2 * heads
    q = (q_ref[...] * scale).astype(BF16)

    @pl.when(j == 0)
    def _():
        kn = kn_ref[...].astype(BF16).astype(F32)
        qn = q.astype(F32) * jnp.concatenate([kn, kn], axis=0)
        m_s[...] = jnp.sum(qn, axis=1, keepdims=True)
        l_s[...] = jnp.ones(l_s.shape, F32)
        vn = vn_ref[...].astype(BF16).astype(F32)
        acc_s[...] = jnp.concatenate([vn, vn], axis=0)

    s = _wide_scores(q, k_refs)
    own = _own_head(rows, s.shape[1], heads)
    first = (n_pages - (j + 1) * g_pages) * page
    col = lax.broadcasted_iota(jnp.int32, s.shape, 1)
    dist = (n_pages * page - first - jnp.right_shift(col, heads.bit_length() - 1)).astype(F32)
    s = jnp.where(own, s - slope_ref[...] * dist, NEG)
    m_old = m_s[...]
    m_new = jnp.maximum(m_old, jnp.max(s, axis=1, keepdims=True))
    alpha = jnp.exp(m_old - m_new)
    pr = jnp.exp(s - m_new)
    l_s[...] = alpha * l_s[...] + jnp.sum(pr, axis=1, keepdims=True)
    acc_s[...] = alpha * acc_s[...] + _wide_pv(pr.astype(BF16), v_refs)
    m_s[...] = m_new

    @pl.when(j == pl.num_programs(1) - 1)
    def _():
        lam = _lambda_value(lq1_ref, lk1_ref, lq2_ref, lk2_ref, lambda_init)
        on = acc_s[...] / l_s[...]
        o = on[:heads] - lam * on[heads:]
        o = o * lax.rsqrt(jnp.mean(o * o, axis=1, keepdims=True) + LN_EPS) * g_ref[...]
        o_ref[...] = (o * (1.0 - lambda_init)).astype(o_ref.dtype)


def _page_specs(n_pages, g_pages, block, layer):
    def make(g):
        def index(b, j, pt):
            return (layer, pt[b, n_pages - (j + 1) * g_pages + g]) + (0,) * len(block)
        return pl.BlockSpec((None, None) + block, index)
    return [make(g) for g in range(g_pages)]


def _per_seq_spec(shape):
    return pl.BlockSpec((None,) + shape, lambda b, j, pt: (b,) + (0,) * len(shape))


def _shared_spec(shape):
    return pl.BlockSpec(shape, lambda b, j, pt: (0,) * len(shape))


def _fox_decode(page_table, q, kn, vn, lfn, lf_cache, k_cache, v_cache, layer, g_pages=16):
    bsz, n_pages = page_table.shape
    _, _, page, heads, dh = k_cache.shape
    g_pages = _tile(n_pages, g_pages)
    grid_spec = pltpu.PrefetchScalarGridSpec(
        num_scalar_prefetch=1,
        grid=(bsz, n_pages // g_pages),
        in_specs=([_per_seq_spec((heads, dh))] * 3 + [_per_seq_spec((heads, 1))]
                  + _page_specs(n_pages, g_pages, (1, page * heads), layer)
                  + _page_specs(n_pages, g_pages, (page, heads, dh), layer)
                  + _page_specs(n_pages, g_pages, (page, heads, dh), layer)),
        out_specs=_per_seq_spec((heads, dh)),
        scratch_shapes=[pltpu.VMEM((heads, 1), F32), pltpu.VMEM((heads, 1), F32),
                        pltpu.VMEM((heads, dh), F32), pltpu.VMEM((heads, 1), F32)])
    return pl.pallas_call(
        functools.partial(_fox_dec_kernel, g_pages=g_pages, scale=dh ** -0.5),
        grid_spec=grid_spec,
        out_shape=jax.ShapeDtypeStruct((bsz, heads, dh), BF16),
        compiler_params=_cparams(("parallel", "arbitrary")),
        name="fox_decode",
    )(page_table, q, kn, vn, lfn, *([lf_cache] * g_pages), *([k_cache] * g_pages), *([v_cache] * g_pages))


def _sb_decode(page_table, q, k_cache, v_cache, layer, g_pages=16):
    bsz, n_pages = page_table.shape
    _, _, page, heads, dh = k_cache.shape
    g_pages = _tile(n_pages, g_pages)
    grid_spec = pltpu.PrefetchScalarGridSpec(
        num_scalar_prefetch=1,
        grid=(bsz, n_pages // g_pages),
        in_specs=([_per_seq_spec((heads, dh))]
                  + _page_specs(n_pages, g_pages, (page, heads, dh), layer)
                  + _page_specs(n_pages, g_pages, (page, heads, dh), layer)),
        out_specs=_per_seq_spec((heads, dh)),
        scratch_shapes=[pltpu.VMEM((heads, dh), F32), pltpu.VMEM((heads, 1), F32)])
    return pl.pallas_call(
        functools.partial(_sb_dec_kernel, g_pages=g_pages, scale=dh ** -0.5),
        grid_spec=grid_spec,
        out_shape=jax.ShapeDtypeStruct((bsz, heads, dh), BF16),
        compiler_params=_cparams(("parallel", "arbitrary")),
        name="sb_decode",
    )(page_table, q, *([k_cache] * g_pages), *([v_cache] * g_pages))


def _diff_decode(page_table, q, kn, vn, slopes, lq1, lk1, lq2, lk2, g, k_cache, v_cache, layer, lambda_init,
                 g_pages=8):
    bsz, n_pages = page_table.shape
    _, _, page, heads, dv = k_cache.shape
    dq = dv // 2
    rows = 2 * heads
    g_pages = _tile(n_pages, g_pages)
    grid_spec = pltpu.PrefetchScalarGridSpec(
        num_scalar_prefetch=1,
        grid=(bsz, n_pages // g_pages),
        in_specs=([_per_seq_spec((rows, dv)), _per_seq_spec((heads, dv)), _per_seq_spec((heads, dv)),
                   _shared_spec((rows, 1)),
                   _shared_spec((1, dq)), _shared_spec((1, dq)), _shared_spec((1, dq)), _shared_spec((1, dq)),
                   _shared_spec((1, dv))]
                  + _page_specs(n_pages, g_pages, (page, heads, dv), layer)
                  + _page_specs(n_pages, g_pages, (page, heads, dv), layer)),
        out_specs=_per_seq_spec((heads, dv)),
        scratch_shapes=[pltpu.VMEM((rows, 1), F32), pltpu.VMEM((rows, 1), F32),
                        pltpu.VMEM((rows, dv), F32)])
    return pl.pallas_call(
        functools.partial(_diff_dec_kernel, g_pages=g_pages, scale=dq ** -0.5, lambda_init=lambda_init,
                          n_pages=n_pages),
        grid_spec=grid_spec,
        out_shape=jax.ShapeDtypeStruct((bsz, heads, dv), BF16),
        compiler_params=_cparams(("parallel", "arbitrary")),
        name="diff_decode",
    )(page_table, q, kn, vn, slopes, lq1, lk1, lq2, lk2, g, *([k_cache] * g_pages), *([v_cache] * g_pages))


def _even_layer(xp, xpb, xsb, caches, page_table, w_in, b_f, w_o, dims):
    cache_k_a, cache_v_a, cache_lf, cache_k_b, cache_v_b, layer_idx = caches
    bsz, t, dbs, heads, dh = dims
    w = heads * dh
    scale = dh ** -0.5
    wb = w_in[:, :3 * w].astype(BF16)
    wb_b = w_in[:, 3 * w + heads:].astype(BF16)
    w_f = jnp.pad(w_in[:, 3 * w:3 * w + heads].astype(BF16), ((0, 0), (0, LANES - heads)))
    b_pad = jnp.pad(b_f.astype(F32), (0, LANES - heads)).reshape(1, LANES)
    wo_a, wo_b = w_o[:w].astype(BF16), w_o[w:].astype(BF16)

    if xpb is None:
        q_a, xpb, lf = _entry_projection(xp, wb, w, scale * LOG2E, w_f, b_pad)
    else:
        (q_a,) = _mm(xpb, wb, (BF16,), out_scale=scale * LOG2E, col0=0, n=w)
        lf = _forget_gate(xpb, w_f, b_pad)
    lf = lf[:, :heads]
    k_a, k_ab = _mm(xpb, wb, (F32, BF16), col0=w, n=w)
    v_a, v_ab = _mm(xpb, wb, (F32, BF16), col0=2 * w, n=w)
    (q_b,) = _mm(xpb, wb_b, (BF16,), out_scale=scale * LOG2E, col0=0, n=w)
    k_b, k_bb = _mm(xpb, wb_b, (F32, BF16), col0=w, n=w)
    v_b, v_bb = _mm(xpb, wb_b, (F32, BF16), col0=2 * w, n=w)
    lf_t = jnp.swapaxes(lf.reshape(bsz, t, heads), 1, 2)
    cum = _suffix_sums(lf_t.reshape(bsz * heads, t), LOG2E).reshape(bsz, heads, 1, t)
    o_a = _fox_attention(q_a, k_ab, v_ab, cum, bsz, heads, dh)
    o_b = _sb_attention(q_b, k_bb, v_bb, bsz, heads, dh)
    new_p = (k_a.reshape(bsz, t, heads, dh), v_a.reshape(bsz, t, heads, dh), lf.reshape(bsz, t, heads),
             k_b.reshape(bsz, t, heads, dh), v_b.reshape(bsz, t, heads, dh))

    (qkv_a,) = _mm(xsb, wb, (F32,), col0=0, n=3 * w)
    (qkv_b,) = _mm(xsb, wb_b, (F32,), col0=0, n=3 * w)
    sq_a, sk_a, sv_a = (qkv_a[:, i * w:(i + 1) * w] for i in range(3))
    sq_b, sk_b, sv_b = (qkv_b[:, i * w:(i + 1) * w] for i in range(3))
    slf = _forget_gate(xsb, w_f, b_pad)[:, :heads]
    per_head = lambda a: a.reshape(dbs, heads, dh)
    n_l, n_pool, page, _ = cache_lf.shape
    lf_flat = cache_lf.reshape(n_l, n_pool, 1, page * heads)
    so_a = _fox_decode(page_table, per_head(sq_a), per_head(sk_a), per_head(sv_a), slf.reshape(dbs, heads, 1),
                       lf_flat, cache_k_a, cache_v_a, layer_idx)
    so_b = _sb_decode(page_table, per_head(sq_b), cache_k_b, cache_v_b, layer_idx)
    new_s = (sk_a.reshape(dbs, 1, heads, dh), sv_a.reshape(dbs, 1, heads, dh), slf.reshape(dbs, 1, heads),
             sk_b.reshape(dbs, 1, heads, dh), sv_b.reshape(dbs, 1, heads, dh))
    return (o_a, o_b), (so_a.reshape(dbs, w), so_b.reshape(dbs, w)), (wo_a, wo_b), new_p, new_s


def _odd_layer(xpb, xsb, caches, page_table, w_in, lq1, lk1, lq2, lk2, gain, w_o, lambda_init, dims):
    cache_k, cache_v, layer_idx = caches
    bsz, t, dbs, heads, dq = dims
    dv = 2 * dq
    w = heads * dv
    scale = dq ** -0.5
    wb = w_in.astype(BF16)
    slopes = 2.0 ** (-8.0 * jnp.arange(1, heads + 1, dtype=F32) / heads)
    vecs = [a.astype(F32).reshape(1, dq) for a in (lq1, lk1, lq2, lk2)]
    g_row = gain.astype(F32).reshape(1, dv)

    (q,) = _mm(xpb, wb, (BF16,), out_scale=scale * LOG2E, col0=0, n=w)
    k, kb = _mm(xpb, wb, (F32, BF16), col0=w, n=w)
    v, vb = _mm(xpb, wb, (F32, BF16), col0=2 * w, n=w)
    o = _diff_attention(q, kb, vb, slopes * LOG2E, *vecs, g_row, bsz, heads, dq, lambda_init)
    new_p = (k.reshape(bsz, t, heads, dv), v.reshape(bsz, t, heads, dv))

    (qkv,) = _mm(xsb, wb, (F32,), col0=0, n=3 * w)
    sq, sk, sv = (qkv[:, i * w:(i + 1) * w] for i in range(3))
    sq4 = sq.reshape(dbs, heads, 2, dq)
    zero = jnp.zeros((dbs, heads, dq), F32)
    q_rows = jnp.concatenate([jnp.concatenate([sq4[:, :, 0], zero], axis=2),
                              jnp.concatenate([zero, sq4[:, :, 1]], axis=2)], axis=1)
    slope_rows = jnp.concatenate([slopes, slopes]).reshape(2 * heads, 1)
    so = _diff_decode(page_table, q_rows, sk.reshape(dbs, heads, dv), sv.reshape(dbs, heads, dv), slope_rows,
                      *vecs, g_row, cache_k, cache_v, layer_idx, lambda_init)
    new_s = (sk.reshape(dbs, 1, heads, dv), sv.reshape(dbs, 1, heads, dv))
    return (o,), (so.reshape(dbs, w),), (w_o.astype(BF16),), new_p, new_s


def kernel(x_prompt, x_sample, cache_fox_k, cache_fox_v, cache_fox_logf, cache_sb_k, cache_sb_v, cache_diff_k, cache_diff_v, page_table, w_in_even, b_forget, w_out_even, w_in_odd, lambda_q1, lambda_k1, lambda_q2, lambda_k2, subln_gain, w_out_odd, ln_mix_gain, ln_mix_bias, ln_ffn_gain, ln_ffn_bias, w_ffn_up, w_ffn_down):
    bsz, t, d = x_prompt.shape
    dbs, ts, _ = x_sample.shape
    assert ts == 1, "the sample group carries one new token per sequence"
    depth = ln_mix_gain.shape[0]
    alpha = (2 * depth) ** 0.25
    h_fox, dh = cache_fox_k.shape[3], cache_fox_k.shape[4]
    h_diff, dq = cache_diff_k.shape[3], cache_diff_k.shape[4] // 2
    assert cache_sb_k.shape[3:] == (h_fox, dh), "both head groups of an even layer share one geometry"

    xp = x_prompt.reshape(bsz * t, d)
    xs = x_sample.reshape(dbs, d)
    xpb, xsb = None, xs.astype(BF16)
    even_p, even_s, odd_p, odd_s = [], [], [], []
    row = lambda a: a.astype(F32).reshape(1, d)
    for layer in range(depth):
        i = layer // 2
        if layer % 2 == 0:
            caches = (cache_fox_k, cache_fox_v, cache_fox_logf, cache_sb_k, cache_sb_v, i)
            o_p, o_s, w_o, new_p, new_s = _even_layer(xp, xpb, xsb, caches, page_table, w_in_even[i],
                                                      b_forget[i], w_out_even[i], (bsz, t, dbs, h_fox, dh))
            even_p.append(new_p)
            even_s.append(new_s)
        else:
            lambda_init = 0.8 - 0.6 * math.exp(-0.3 * layer)
            caches = (cache_diff_k, cache_diff_v, i)
            o_p, o_s, w_o, new_p, new_s = _odd_layer(xpb, xsb, caches, page_table, w_in_odd[i], lambda_q1[i],
                                                     lambda_k1[i], lambda_q2[i], lambda_k2[i], subln_gain[i],
                                                     w_out_odd[i], lambda_init, (bsz, t, dbs, h_diff, dq))
            odd_p.append(new_p)
            odd_s.append(new_s)
        g1, b1 = row(ln_mix_gain[layer]), row(ln_mix_bias[layer])
        g2, b2 = row(ln_ffn_gain[layer]), row(ln_ffn_bias[layer])
        xp, xpb = _proj_residual_ln(list(o_p), list(w_o), xp, g1, b1, alpha)
        xs, xsb = _proj_residual_ln(list(o_s), list(w_o), xs, g1, b1, alpha)
        xs, xsb, w_up, w_down = _ffn_residual_ln_casting(xsb, xs, w_ffn_up, w_ffn_down, layer, g2, b2, alpha)
        xp, xpb = _ffn_residual_ln(xpb, xp, w_up, w_down, g2, b2, alpha)

    def stk(rows, j):
        return jnp.stack([r[j] for r in rows])

    return (xp.reshape(bsz, t, d), xs.reshape(dbs, 1, d),
            stk(even_p, 0), stk(even_p, 1), stk(even_p, 2), stk(even_p, 3), stk(even_p, 4), stk(odd_p, 0), stk(odd_p, 1),
            stk(even_s, 0), stk(even_s, 1), stk(even_s, 2), stk(even_s, 3), stk(even_s, 4), stk(odd_s, 0), stk(odd_s, 1))
```

```python
import functools
import math

import jax
import jax.numpy as jnp
from jax import lax
from jax.experimental import pallas as pl
from jax.experimental.pallas import tpu as pltpu

F32 = jnp.float32
BF16 = jnp.bfloat16
LN_EPS = 1e-5
NEG = -1e30
LOG2E = 1.4426950408889634
LANES = 128
CUM_CHUNK = 256
DIAG_STRIP = 256
VMEM_LIMIT = 56 * 1024 * 1024


def _cparams(sem):
    return pltpu.CompilerParams(dimension_semantics=sem, vmem_limit_bytes=VMEM_LIMIT)


def _tile(n, pref):
    t = min(n, pref)
    while n % t:
        t //= 2
    return t


def _dot(a, b):
    return jnp.dot(a, b, preferred_element_type=F32)


def _dot_nt(a, b):
    return lax.dot_general(a, b, (((1,), (1,)), ((), ())), preferred_element_type=F32)


def _bf16_terms(x, terms):
    out = []
    r = x
    for t in range(terms):
        h = r.astype(BF16)
        out.append(h)
        if t + 1 < terms:
            r = r - h.astype(F32)
    return out


def _split_dot(x, u, terms):
    out = None
    for h in _bf16_terms(x, terms):
        d = _dot(h, u)
        out = d if out is None else out + d
    return out


def _suffix_matrix(n):
    r = lax.broadcasted_iota(jnp.int32, (n, n), 0)
    c = lax.broadcasted_iota(jnp.int32, (n, n), 1)
    return jnp.where(r > c, 1.0, 0.0).astype(BF16)


def _log_sigmoid(x):
    return jnp.minimum(x, 0.0) - jnp.log1p(jnp.exp(-jnp.abs(x)))


def _layer_norm(x, g, b):
    mu = jnp.mean(x, axis=-1, keepdims=True)
    xc = x - mu
    var = jnp.mean(xc * xc, axis=-1, keepdims=True)
    return xc * lax.rsqrt(var + LN_EPS) * g + b


def _mm_kernel(x_ref, w_ref, *o_refs, out_scale):
    acc = _dot(x_ref[...], w_ref[...])
    if out_scale != 1.0:
        acc = acc * out_scale
    for o_ref in o_refs:
        o_ref[...] = acc.astype(o_ref.dtype)


def _mm(x, w, out_dtypes, out_scale=1.0, col0=0, n=None, tm=1024, tn=2048):
    m, k = x.shape
    n = w.shape[1] - col0 if n is None else n
    tm, tn = _tile(m, tm), _tile(n, tn)
    assert col0 % tn == 0, "the column window must start on a column-block boundary"
    j0 = col0 // tn
    outs = pl.pallas_call(
        functools.partial(_mm_kernel, out_scale=out_scale),
        grid=(n // tn, m // tm),
        in_specs=[pl.BlockSpec((tm, k), lambda j, i: (i, 0)),
                  pl.BlockSpec((k, tn), lambda j, i: (0, j0 + j))],
        out_specs=[pl.BlockSpec((tm, tn), lambda j, i: (i, j)) for _ in out_dtypes],
        out_shape=[jax.ShapeDtypeStruct((m, n), d) for d in out_dtypes],
        compiler_params=_cparams(("parallel", "parallel")),
        name="proj",
    )(x, w)
    return outs


def _entry_kernel(x_ref, w_ref, wf_ref, b_ref, q_ref, xb_ref, gate_ref, *, out_scale):
    xb = x_ref[...].astype(BF16)
    xb_ref[...] = xb
    q_ref[...] = (_dot(xb, w_ref[...]) * out_scale).astype(q_ref.dtype)
    gate_ref[...] = _log_sigmoid(_dot(xb, wf_ref[...]) + b_ref[...])


def _entry_projection(x, w, n, out_scale, w_f, b, tm=1024):
    m, k = x.shape
    tm = _tile(m, tm)
    nf = w_f.shape[1]
    return pl.pallas_call(
        functools.partial(_entry_kernel, out_scale=out_scale),
        grid=(m // tm,),
        in_specs=[pl.BlockSpec((tm, k), lambda i: (i, 0)),
                  pl.BlockSpec((k, n), lambda i: (0, 0)),
                  pl.BlockSpec((k, nf), lambda i: (0, 0)),
                  pl.BlockSpec((1, nf), lambda i: (0, 0))],
        out_specs=[pl.BlockSpec((tm, n), lambda i: (i, 0)), pl.BlockSpec((tm, k), lambda i: (i, 0)),
                   pl.BlockSpec((tm, nf), lambda i: (i, 0))],
        out_shape=[jax.ShapeDtypeStruct((m, n), BF16), jax.ShapeDtypeStruct((m, k), BF16),
                   jax.ShapeDtypeStruct((m, nf), F32)],
        compiler_params=_cparams(("parallel",)),
        name="entry_projection",
    )(x, w, w_f, b)


def _gate_kernel(x_ref, w_ref, b_ref, o_ref):
    o_ref[...] = _log_sigmoid(_dot(x_ref[...], w_ref[...]) + b_ref[...])


def _forget_gate(x, w, b, tm=512):
    m, k = x.shape
    n = w.shape[1]
    tm = _tile(m, tm)
    return pl.pallas_call(
        _gate_kernel,
        grid=(m // tm,),
        in_specs=[pl.BlockSpec((tm, k), lambda i: (i, 0)),
                  pl.BlockSpec((k, n), lambda i: (0, 0)),
                  pl.BlockSpec((1, n), lambda i: (0, 0))],
        out_specs=pl.BlockSpec((tm, n), lambda i: (i, 0)),
        out_shape=jax.ShapeDtypeStruct((m, n), F32),
        compiler_params=_cparams(("parallel",)),
        name="forget_gate",
    )(x, w, b)


def _revcum_kernel(x_ref, o_ref, *, out_scale):
    rows, t = x_ref.shape
    ch = min(CUM_CHUNK, t)
    u = _suffix_matrix(ch)
    carry = jnp.zeros((rows, 1), F32)
    for c in reversed(range(t // ch)):
        xc = x_ref[:, c * ch:(c + 1) * ch]
        o_ref[:, c * ch:(c + 1) * ch] = (_split_dot(xc, u, 3) + carry) * out_scale
        carry = carry + jnp.sum(xc, axis=1, keepdims=True)


def _suffix_sums(x, out_scale):
    return pl.pallas_call(
        functools.partial(_revcum_kernel, out_scale=out_scale),
        out_shape=jax.ShapeDtypeStruct(x.shape, F32),
        name="suffix_sums",
    )(x)


def _proj_ln_kernel(*refs, n_pairs, alpha):
    a_refs = refs[:n_pairs]
    w_refs = refs[n_pairs:2 * n_pairs]
    x_ref, g_ref, b_ref, y_ref, yb_ref = refs[2 * n_pairs:]
    acc = alpha * x_ref[...]
    for a_ref, w_ref in zip(a_refs, w_refs):
        acc = acc + _dot(a_ref[...], w_ref[...])
    y = _layer_norm(acc, g_ref[...], b_ref[...])
    y_ref[...] = y
    yb_ref[...] = y.astype(BF16)


def _proj_residual_ln(a_list, w_list, x, g, b, alpha, tm=512):
    m, d = x.shape
    tm = _tile(m, tm)
    n_pairs = len(a_list)
    in_specs = ([pl.BlockSpec((tm, a.shape[1]), lambda i: (i, 0)) for a in a_list]
                + [pl.BlockSpec(w.shape, lambda i: (0, 0)) for w in w_list]
                + [pl.BlockSpec((tm, d), lambda i: (i, 0)),
                   pl.BlockSpec((1, d), lambda i: (0, 0)),
                   pl.BlockSpec((1, d), lambda i: (0, 0))])
    return pl.pallas_call(
        functools.partial(_proj_ln_kernel, n_pairs=n_pairs, alpha=alpha),
        grid=(m // tm,),
        in_specs=in_specs,
        out_specs=[pl.BlockSpec((tm, d), lambda i: (i, 0)), pl.BlockSpec((tm, d), lambda i: (i, 0))],
        out_shape=[jax.ShapeDtypeStruct((m, d), F32), jax.ShapeDtypeStruct((m, d), BF16)],
        compiler_params=_cparams(("parallel",)),
        name="out_proj_ln",
    )(*a_list, *w_list, x, g, b)


def _ffn_step(f, n_f, xb_ref, x_ref, load_weights, g_ref, b_ref, y_ref, yb_ref, acc_ref, alpha):
    @pl.when(f == 0)
    def _():
        acc_ref[...] = alpha * x_ref[...]

    wu, wd = load_weights()
    h = jnp.maximum(_dot(xb_ref[...], wu), 0.0)
    acc_ref[...] += _dot((h * h).astype(BF16), wd)

    @pl.when(f == n_f - 1)
    def _():
        y = _layer_norm(acc_ref[...], g_ref[...], b_ref[...])
        y_ref[...] = y
        yb_ref[...] = y.astype(BF16)


def _ffn_kernel(xb_ref, x_ref, wu_ref, wd_ref, g_ref, b_ref, y_ref, yb_ref, acc_ref, *, alpha):
    _ffn_step(pl.program_id(1), pl.num_programs(1), xb_ref, x_ref, lambda: (wu_ref[...], wd_ref[...]), g_ref, b_ref,
              y_ref, yb_ref, acc_ref, alpha)


def _ffn_residual_ln(xb, x, w_up, w_down, g, b, alpha, tm=512, tf=1024):
    m, d = x.shape
    dff = w_up.shape[1]
    tm, tf = _tile(m, tm), _tile(dff, tf)
    return pl.pallas_call(
        functools.partial(_ffn_kernel, alpha=alpha),
        grid=(m // tm, dff // tf),
        in_specs=[pl.BlockSpec((tm, d), lambda i, f: (i, 0)),
                  pl.BlockSpec((tm, d), lambda i, f: (i, 0)),
                  pl.BlockSpec((d, tf), lambda i, f: (0, f)),
                  pl.BlockSpec((tf, d), lambda i, f: (f, 0)),
                  pl.BlockSpec((1, d), lambda i, f: (0, 0)),
                  pl.BlockSpec((1, d), lambda i, f: (0, 0))],
        out_specs=[pl.BlockSpec((tm, d), lambda i, f: (i, 0)), pl.BlockSpec((tm, d), lambda i, f: (i, 0))],
        out_shape=[jax.ShapeDtypeStruct((m, d), F32), jax.ShapeDtypeStruct((m, d), BF16)],
        scratch_shapes=[pltpu.VMEM((tm, d), F32)],
        compiler_params=_cparams(("parallel", "arbitrary")),
        name="ffn_ln",
    )(xb, x, w_up, w_down, g, b)


def _ffn_cast_kernel(xb_ref, x_ref, wu_ref, wd_ref, g_ref, b_ref, y_ref, yb_ref, wub_ref, wdb_ref, acc_ref, *, alpha):
    def cast_weights():
        wu = wu_ref[...].astype(BF16)
        wd = wd_ref[...].astype(BF16)
        wub_ref[...] = wu
        wdb_ref[...] = wd
        return wu, wd

    _ffn_step(pl.program_id(0), pl.num_programs(0), xb_ref, x_ref, cast_weights, g_ref, b_ref, y_ref, yb_ref,
              acc_ref, alpha)


def _ffn_residual_ln_casting(xb, x, w_up, w_down, layer, g, b, alpha, tf=512):
    m, d = x.shape
    dff = w_up.shape[2]
    tf = _tile(dff, tf)
    return pl.pallas_call(
        functools.partial(_ffn_cast_kernel, alpha=alpha),
        grid=(dff // tf,),
        in_specs=[pl.BlockSpec((m, d), lambda f: (0, 0)),
                  pl.BlockSpec((m, d), lambda f: (0, 0)),
                  pl.BlockSpec((None, d, tf), lambda f: (layer, 0, f)),
                  pl.BlockSpec((None, tf, d), lambda f: (layer, f, 0)),
                  pl.BlockSpec((1, d), lambda f: (0, 0)),
                  pl.BlockSpec((1, d), lambda f: (0, 0))],
        out_specs=[pl.BlockSpec((m, d), lambda f: (0, 0)), pl.BlockSpec((m, d), lambda f: (0, 0)),
                   pl.BlockSpec((d, tf), lambda f: (0, f)), pl.BlockSpec((tf, d), lambda f: (f, 0))],
        out_shape=[jax.ShapeDtypeStruct((m, d), F32), jax.ShapeDtypeStruct((m, d), BF16),
                   jax.ShapeDtypeStruct((d, dff), BF16), jax.ShapeDtypeStruct((dff, d), BF16)],
        scratch_shapes=[pltpu.VMEM((m, d), F32)],
        compiler_params=_cparams(("arbitrary",)),
        name="ffn_ln_casting",
    )(xb, x, w_up, w_down, g, b)


def _pair_tables(n_blocks, descending):
    qi, kj = [], []
    for q in range(n_blocks):
        ks = range(q, -1, -1) if descending else range(q + 1)
        for k in ks:
            qi.append(q)
            kj.append(k)
    return jnp.asarray(qi, jnp.int32), jnp.asarray(kj, jnp.int32)


def _tri_iota(t):
    r = lax.broadcasted_iota(jnp.int32, (t, t), 0)
    c = lax.broadcasted_iota(jnp.int32, (t, t), 1)
    return r, c


def _causal(rows, width, row0=0, col0=0, strict=False):
    r = jnp.tile(lax.broadcasted_iota(jnp.int32, (rows, LANES), 0), (1, width // LANES)) + row0
    c = lax.broadcasted_iota(jnp.int32, (1, width), 1) + col0
    return (c < r) if strict else (c <= r)


def _row_strips(t, diagonal, rows=DIAG_STRIP):
    if not diagonal:
        return [(0, t, t)]
    rows = min(rows, t)
    return [(r0, r0 + rows, r0 + rows) for r0 in range(0, t, rows)]


def _lanes_to_rows(row, t):
    r, c = _tri_iota(t)
    col = jnp.sum(jnp.where(r == c, row, 0.0), axis=1, keepdims=True)
    return jnp.broadcast_to(col, (t, LANES))


def _softmax_step(a, row_shift, m_prev, l_prev, acc_prev, v):
    t, tk = a.shape
    m_next = jnp.maximum(m_prev, jnp.max(a, axis=1, keepdims=True) - row_shift)
    pr = jnp.exp2(a - jnp.tile(m_next + row_shift, (1, tk // LANES)))
    alpha = jnp.exp2(m_prev - m_next)
    l_next = alpha * l_prev + jnp.sum(pr, axis=1, keepdims=True)
    acc_next = jnp.tile(alpha, (1, v.shape[1] // LANES)) * acc_prev + _dot(pr.astype(BF16), v)
    return m_next, l_next, acc_next


def _fox_kernel(qi_tab, kj_tab, q_ref, k_ref, v_ref, ck_ref, cqrow_ref, o_ref, m_s, l_s, acc_s, cq_s, *, dh):
    p = pl.program_id(2)
    qi, kj = qi_tab[p], kj_tab[p]
    t = q_ref.shape[0]
    hb = m_s.shape[0]

    @pl.when(kj == 0)
    def _():
        m_s[...] = jnp.full(m_s.shape, -jnp.inf, F32)
        l_s[...] = jnp.zeros(l_s.shape, F32)
        acc_s[...] = jnp.zeros(acc_s.shape, F32)
        for i in range(hb):
            cq_s[i] = _lanes_to_rows(cqrow_ref[i], t)

    def step(diagonal):
        for i in range(hb):
            cols = slice(i * dh, (i + 1) * dh)
            for r0, r1, kc in _row_strips(t, diagonal):
                a = _dot_nt(q_ref[r0:r1, cols], k_ref[:kc, cols]) + ck_ref[i, :, :kc]
                if diagonal:
                    a = jnp.where(_causal(r1 - r0, kc, row0=r0), a, NEG)
                m, l, acc = _softmax_step(a, cq_s[i, r0:r1], m_s[i, r0:r1], l_s[i, r0:r1], acc_s[i, r0:r1],
                                          v_ref[:kc, cols])
                m_s[i, r0:r1] = m
                l_s[i, r0:r1] = l
                acc_s[i, r0:r1] = acc

    @pl.when(kj < qi)
    def _():
        step(False)

    @pl.when(kj == qi)
    def _():
        step(True)
        for i in range(hb):
            o_ref[:, i * dh:(i + 1) * dh] = (acc_s[i] / l_s[i]).astype(o_ref.dtype)


def _fox_attention(q, k, v, cum, batch, heads, dh, tb=512, hb=4):
    assert dh == LANES
    m = q.shape[0]
    t = m // batch
    tb = _tile(t, tb)
    hb = _tile(heads, hb)
    nb = t // tb
    qi_tab, kj_tab = _pair_tables(nb, descending=False)
    grid_spec = pltpu.PrefetchScalarGridSpec(
        num_scalar_prefetch=2,
        grid=(batch, heads // hb, int(qi_tab.shape[0])),
        in_specs=[pl.BlockSpec((tb, hb * dh), lambda b, h, p, qt, kt: (b * nb + qt[p], h)),
                  pl.BlockSpec((tb, hb * dh), lambda b, h, p, qt, kt: (b * nb + kt[p], h)),
                  pl.BlockSpec((tb, hb * dh), lambda b, h, p, qt, kt: (b * nb + kt[p], h)),
                  pl.BlockSpec((None, hb, 1, tb), lambda b, h, p, qt, kt: (b, h, 0, kt[p])),
                  pl.BlockSpec((None, hb, 1, tb), lambda b, h, p, qt, kt: (b, h, 0, qt[p]))],
        out_specs=pl.BlockSpec((tb, hb * dh), lambda b, h, p, qt, kt: (b * nb + qt[p], h)),
        scratch_shapes=[pltpu.VMEM((hb, tb, LANES), F32), pltpu.VMEM((hb, tb, LANES), F32),
                        pltpu.VMEM((hb, tb, dh), F32), pltpu.VMEM((hb, tb, LANES), F32)])
    return pl.pallas_call(
        functools.partial(_fox_kernel, dh=dh),
        grid_spec=grid_spec,
        out_shape=jax.ShapeDtypeStruct((m, heads * dh), BF16),
        compiler_params=_cparams(("parallel", "parallel", "arbitrary")),
        name="fox_attention",
    )(qi_tab, kj_tab, q, k, v, cum, cum)


def _sb_kernel(qi_tab, kj_tab, q_ref, k_ref, v_ref, o_ref, carry_s, acc_s, *, dh):
    p = pl.program_id(2)
    qi, kj = qi_tab[p], kj_tab[p]
    t = q_ref.shape[0]
    ch = min(CUM_CHUNK, t)
    hb = acc_s.shape[0]

    @pl.when(kj == qi)
    def _():
        carry_s[...] = jnp.zeros(carry_s.shape, F32)
        acc_s[...] = jnp.zeros(acc_s.shape, F32)

    def step(diagonal):
        u = _suffix_matrix(ch)
        for i in range(hb):
            cols = slice(i * dh, (i + 1) * dh)
            for r0, r1, kc in _row_strips(t, diagonal, ch):
                drop = carry_s[i, r0:r1]
                acc = acc_s[i, r0:r1]
                for c0 in reversed(range(0, kc, ch)):
                    z = _dot_nt(q_ref[r0:r1, cols], k_ref[c0:c0 + ch, cols])
                    sp = jnp.maximum(z, 0.0) + jnp.log2(1.0 + jnp.exp2(-jnp.abs(z)))
                    on_diagonal = diagonal and c0 + ch > r0
                    if on_diagonal:
                        strict = _causal(r1 - r0, ch, row0=r0, col0=c0, strict=True)
                        sp = jnp.where(strict, sp, 0.0)
                    hi, lo = _bf16_terms(sp, 2)
                    a = jnp.exp2(z - sp - (_dot(hi, u) + _dot(lo, u)) - jnp.tile(drop, (1, ch // LANES)))
                    if on_diagonal:
                        a = jnp.where(strict, a, 0.0)
                    acc = acc + _dot(a.astype(BF16), v_ref[c0:c0 + ch, cols])
                    drop = drop + jnp.sum(sp, axis=1, keepdims=True)
                carry_s[i, r0:r1] = drop
                acc_s[i, r0:r1] = acc

    @pl.when(kj == qi)
    def _():
        step(True)

    @pl.when(kj < qi)
    def _():
        step(False)

    @pl.when(kj == 0)
    def _():
        for i in range(hb):
            o_ref[:, i * dh:(i + 1) * dh] = acc_s[i].astype(o_ref.dtype)


def _sb_attention(q, k, v, batch, heads, dh, tb=512, hb=8):
    m = q.shape[0]
    t = m // batch
    tb = _tile(t, tb)
    hb = _tile(heads, hb)
    nb = t // tb
    qi_tab, kj_tab = _pair_tables(nb, descending=True)
    grid_spec = pltpu.PrefetchScalarGridSpec(
        num_scalar_prefetch=2,
        grid=(batch, heads // hb, int(qi_tab.shape[0])),
        in_specs=[pl.BlockSpec((tb, hb * dh), lambda b, h, p, qt, kt: (b * nb + qt[p], h)),
                  pl.BlockSpec((tb, hb * dh), lambda b, h, p, qt, kt: (b * nb + kt[p], h)),
                  pl.BlockSpec((tb, hb * dh), lambda b, h, p, qt, kt: (b * nb + kt[p], h))],
        out_specs=pl.BlockSpec((tb, hb * dh), lambda b, h, p, qt, kt: (b * nb + qt[p], h)),
        scratch_shapes=[pltpu.VMEM((hb, tb, LANES), F32), pltpu.VMEM((hb, tb, dh), F32)])
    return pl.pallas_call(
        functools.partial(_sb_kernel, dh=dh),
        grid_spec=grid_spec,
        out_shape=jax.ShapeDtypeStruct((m, heads * dh), BF16),
        compiler_params=_cparams(("parallel", "parallel", "arbitrary")),
        name="sb_attention",
    )(qi_tab, kj_tab, q, k, v)


def _lambda_value(lq1_ref, lk1_ref, lq2_ref, lk2_ref, lambda_init):
    a = jnp.sum(lq1_ref[...] * lk1_ref[...], axis=1, keepdims=True)
    b = jnp.sum(lq2_ref[...] * lk2_ref[...], axis=1, keepdims=True)
    return jnp.exp(a) - jnp.exp(b) + lambda_init


def _diff_kernel(qi_tab, kj_tab, slope_ref, q_ref, k_ref, v_ref, lq1_ref, lk1_ref, lq2_ref, lk2_ref,
                 g_ref, o_ref, m_s, l_s, acc_s, *, lambda_init, dq):
    hg = pl.program_id(1)
    p = pl.program_id(2)
    qi, kj = qi_tab[p], kj_tab[p]
    t = q_ref.shape[0]
    dv = 2 * dq
    hb = m_s.shape[0] // 2

    @pl.when(kj == 0)
    def _():
        m_s[...] = jnp.full(m_s.shape, -jnp.inf, F32)
        l_s[...] = jnp.zeros(l_s.shape, F32)
        acc_s[...] = jnp.zeros(acc_s.shape, F32)

    def step(diagonal):
        for j in range(hb):
            slope = slope_ref[hg * hb + j]
            col_bias = slope * lax.broadcasted_iota(jnp.int32, (1, t), 1).astype(F32)
            row_pos = (qi - kj) * t + lax.broadcasted_iota(jnp.int32, (t, LANES), 0)
            row_shift = slope * row_pos.astype(F32)
            for i in range(2):
                cols = slice(j * dv + i * dq, j * dv + (i + 1) * dq)
                s = 2 * j + i
                for r0, r1, kc in _row_strips(t, diagonal):
                    a = _dot_nt(q_ref[r0:r1, cols], k_ref[:kc, cols]) + col_bias[:, :kc]
                    if diagonal:
                        a = jnp.where(_causal(r1 - r0, kc, row0=r0), a, NEG)
                    m, l, acc = _softmax_step(a, row_shift[r0:r1], m_s[s, r0:r1], l_s[s, r0:r1], acc_s[s, r0:r1],
                                              v_ref[:kc, j * dv:(j + 1) * dv])
                    m_s[s, r0:r1] = m
                    l_s[s, r0:r1] = l
                    acc_s[s, r0:r1] = acc

    @pl.when(kj < qi)
    def _():
        step(False)

    @pl.when(kj == qi)
    def _():
        step(True)
        lam = _lambda_value(lq1_ref, lk1_ref, lq2_ref, lk2_ref, lambda_init)
        reps = dv // LANES
        for j in range(hb):
            o = (acc_s[2 * j] / jnp.tile(l_s[2 * j], (1, reps))
                 - lam * (acc_s[2 * j + 1] / jnp.tile(l_s[2 * j + 1], (1, reps))))
            o = o * lax.rsqrt(jnp.mean(o * o, axis=1, keepdims=True) + LN_EPS) * g_ref[...]
            o_ref[:, j * dv:(j + 1) * dv] = (o * (1.0 - lambda_init)).astype(o_ref.dtype)


def _diff_attention(q, k, v, slopes, lq1, lk1, lq2, lk2, g, batch, heads, dq, lambda_init, tb=512, hb=4):
    m = q.shape[0]
    t = m // batch
    tb = _tile(t, tb)
    hb = _tile(heads, hb)
    nb = t // tb
    dv = 2 * dq
    qi_tab, kj_tab = _pair_tables(nb, descending=False)
    vec = pl.BlockSpec((1, dq), lambda b, h, p, qt, kt, sl: (0, 0))
    grid_spec = pltpu.PrefetchScalarGridSpec(
        num_scalar_prefetch=3,
        grid=(batch, heads // hb, int(qi_tab.shape[0])),
        in_specs=[pl.BlockSpec((tb, hb * dv), lambda b, h, p, qt, kt, sl: (b * nb + qt[p], h)),
                  pl.BlockSpec((tb, hb * dv), lambda b, h, p, qt, kt, sl: (b * nb + kt[p], h)),
                  pl.BlockSpec((tb, hb * dv), lambda b, h, p, qt, kt, sl: (b * nb + kt[p], h)),
                  vec, vec, vec, vec,
                  pl.BlockSpec((1, dv), lambda b, h, p, qt, kt, sl: (0, 0))],
        out_specs=pl.BlockSpec((tb, hb * dv), lambda b, h, p, qt, kt, sl: (b * nb + qt[p], h)),
        scratch_shapes=[pltpu.VMEM((2 * hb, tb, LANES), F32), pltpu.VMEM((2 * hb, tb, LANES), F32),
                        pltpu.VMEM((2 * hb, tb, dv), F32)])
    return pl.pallas_call(
        functools.partial(_diff_kernel, lambda_init=lambda_init, dq=dq),
        grid_spec=grid_spec,
        out_shape=jax.ShapeDtypeStruct((m, heads * dv), BF16),
        compiler_params=_cparams(("parallel", "parallel", "arbitrary")),
        name="diff_attention",
    )(qi_tab, kj_tab, slopes, q, k, v, lq1, lk1, lq2, lk2, g)


def _own_head(rows, width, heads):
    assert heads & (heads - 1) == 0
    r = lax.broadcasted_iota(jnp.int32, (rows, width), 0)
    c = lax.broadcasted_iota(jnp.int32, (rows, width), 1)
    return (c & (heads - 1)) == (r & (heads - 1))


def _flat_pages(refs):
    return [ref[...].reshape(ref.shape[0] * ref.shape[1], ref.shape[2]).astype(BF16) for ref in refs]


def _wide_scores(q, k_refs):
    s = [_dot_nt(q, k) for k in _flat_pages(k_refs)]
    return jnp.concatenate(s, axis=1) if len(s) > 1 else s[0]


def _wide_pv(w, v_refs):
    out = None
    width = v_refs[0].shape[0] * v_refs[0].shape[1]
    for g, v in enumerate(_flat_pages(v_refs)):
        d = _dot(w[:, g * width:(g + 1) * width], v)
        out = d if out is None else out + d
    return out


def _wide_suffix_sums(x, carry, terms):
    rows, w = x.shape
    ch = min(CUM_CHUNK, w)
    n = w // ch
    u = _suffix_matrix(ch)
    parts = [p.astype(F32) for p in _bf16_terms(x, terms)]
    stacked = jnp.concatenate([p[:, c * ch:(c + 1) * ch] for c in range(n) for p in parts], axis=0)
    d = _dot(stacked.astype(BF16), u)
    outs = [None] * n
    for c in reversed(range(n)):
        base = c * terms * rows
        tail = d[base:base + rows]
        for t in range(1, terms):
            tail = tail + d[base + t * rows:base + (t + 1) * rows]
        outs[c] = tail + carry
        carry = carry + jnp.sum(x[:, c * ch:(c + 1) * ch], axis=1, keepdims=True)
    return (jnp.concatenate(outs, axis=1) if n > 1 else outs[0]), carry


def _fox_dec_kernel(pt_ref, q_ref, kn_ref, vn_ref, lfn_ref, *refs, g_pages, scale):
    lf_refs = refs[:g_pages]
    k_refs = refs[g_pages:2 * g_pages]
    v_refs = refs[2 * g_pages:3 * g_pages]
    o_ref, m_s, l_s, acc_s, carry_s = refs[3 * g_pages:]
    j = pl.program_id(1)
    heads = q_ref.shape[0]
    q = (q_ref[...] * scale).astype(BF16)

    @pl.when(j == 0)
    def _():
        qn = q.astype(F32) * kn_ref[...].astype(BF16).astype(F32)
        m_s[...] = jnp.sum(qn, axis=1, keepdims=True)
        l_s[...] = jnp.ones(l_s.shape, F32)
        acc_s[...] = vn_ref[...].astype(BF16).astype(F32)
        carry_s[...] = lfn_ref[...]

    s = _wide_scores(q, k_refs)
    own = _own_head(heads, s.shape[1], heads)
    lf = jnp.concatenate([r[...] for r in lf_refs], axis=1) if g_pages > 1 else lf_refs[0][...]
    cum, carry = _wide_suffix_sums(jnp.where(own, lf, 0.0), carry_s[...], 3)
    carry_s[...] = carry
    s = jnp.where(own, s + cum, NEG)
    m_old = m_s[...]
    m_new = jnp.maximum(m_old, jnp.max(s, axis=1, keepdims=True))
    alpha = jnp.exp(m_old - m_new)
    pr = jnp.exp(s - m_new)
    l_s[...] = alpha * l_s[...] + jnp.sum(pr, axis=1, keepdims=True)
    acc_s[...] = alpha * acc_s[...] + _wide_pv(pr.astype(BF16), v_refs)
    m_s[...] = m_new

    @pl.when(j == pl.num_programs(1) - 1)
    def _():
        o_ref[...] = (acc_s[...] / l_s[...]).astype(o_ref.dtype)


def _sb_dec_kernel(pt_ref, q_ref, *refs, g_pages, scale):
    k_refs = refs[:g_pages]
    v_refs = refs[g_pages:2 * g_pages]
    o_ref, acc_s, carry_s = refs[2 * g_pages:]
    j = pl.program_id(1)
    heads = q_ref.shape[0]
    q = (q_ref[...] * scale).astype(BF16)

    @pl.when(j == 0)
    def _():
        acc_s[...] = jnp.zeros(acc_s.shape, F32)
        carry_s[...] = jnp.zeros(carry_s.shape, F32)

    z = _wide_scores(q, k_refs)
    own = _own_head(heads, z.shape[1], heads)
    log_keep = jnp.where(own, _log_sigmoid(-z), 0.0)
    tail, carry = _wide_suffix_sums(log_keep, carry_s[...], 2)
    carry_s[...] = carry
    a = jnp.where(own, jnp.exp(log_keep + z + tail), 0.0)
    acc_s[...] += _wide_pv(a.astype(BF16), v_refs)

    @pl.when(j == pl.num_programs(1) - 1)
    def _():
        o_ref[...] = acc_s[...].astype(o_ref.dtype)


def _diff_dec_kernel(pt_ref, q_ref, kn_ref, vn_ref, slope_ref, lq1_ref, lk1_ref, lq2_ref, lk2_ref,
                     g_ref, *refs, g_pages, scale, lambda_init, n_pages):
    k_refs = refs[:g_pages]
    v_refs = refs[g_pages:2 * g_pages]
    o_ref, m_s, l_s, acc_s = refs[2 * g_pages:]
    j = pl.program_id(1)
    page, heads, _ = k_refs[0].shape
    rows = 2 * heads
    q = (q_ref[...] * scale).astype(BF16)

    @pl.when(j == 0)
    def _():
        kn = kn_ref[...].astype(BF16).astype(F32)
        qn = q.astype(F32) * jnp.concatenate([kn, kn], axis=0)
        m_s[...] = jnp.sum(qn, axis=1, keepdims=True)
        l_s[...] = jnp.ones(l_s.shape, F32)
        vn = vn_ref[...].astype(BF16).astype(F32)
        acc_s[...] = jnp.concatenate([vn, vn], axis=0)

    s = _wide_scores(q, k_refs)
    own = _own_head(rows, s.shape[1], heads)
    first = (n_pages - (j + 1) * g_pages) * page
    col = lax.broadcasted_iota(jnp.int32, s.shape, 1)
    dist = (n_pages * page - first - jnp.right_shift(col, heads.bit_length() - 1)).astype(F32)
    s = jnp.where(own, s - slope_ref[...] * dist, NEG)
    m_old = m_s[...]
    m_new = jnp.maximum(m_old, jnp.max(s, axis=1, keepdims=True))
    alpha = jnp.exp(m_old - m_new)
    pr = jnp.exp(s - m_new)
    l_s[...] = alpha * l_s[...] + jnp.sum(pr, axis=1, keepdims=True)
    acc_s[...] = alpha * acc_s[...] + _wide_pv(pr.astype(BF16), v_refs)
    m_s[...] = m_new

    @pl.when(j == pl.num_programs(1) - 1)
    def _():
        lam = _lambda_value(lq1_ref, lk1_ref, lq2_ref, lk2_ref, lambda_init)
        on = acc_s[...] / l_s[...]
        o = on[:heads] - lam * on[heads:]
        o = o * lax.rsqrt(jnp.mean(o * o, axis=1, keepdims=True) + LN_EPS) * g_ref[...]
        o_ref[...] = (o * (1.0 - lambda_init)).astype(o_ref.dtype)


def _page_specs(n_pages, g_pages, block, layer):
    def make(g):
        def index(b, j, pt):
            return (layer, pt[b, n_pages - (j + 1) * g_pages + g]) + (0,) * len(block)
        return pl.BlockSpec((None, None) + block, index)
    return [make(g) for g in range(g_pages)]


def _per_seq_spec(shape):
    return pl.BlockSpec((None,) + shape, lambda b, j, pt: (b,) + (0,) * len(shape))


def _shared_spec(shape):
    return pl.BlockSpec(shape, lambda b, j, pt: (0,) * len(shape))


def _fox_decode(page_table, q, kn, vn, lfn, lf_cache, k_cache, v_cache, layer, g_pages=16):
    bsz, n_pages = page_table.shape
    _, _, page, heads, dh = k_cache.shape
    g_pages = _tile(n_pages, g_pages)
    grid_spec = pltpu.PrefetchScalarGridSpec(
        num_scalar_prefetch=1,
        grid=(bsz, n_pages // g_pages),
        in_specs=([_per_seq_spec((heads, dh))] * 3 + [_per_seq_spec((heads, 1))]
                  + _page_specs(n_pages, g_pages, (1, page * heads), layer)
                  + _page_specs(n_pages, g_pages, (page, heads, dh), layer)
                  + _page_specs(n_pages, g_pages, (page, heads, dh), layer)),
        out_specs=_per_seq_spec((heads, dh)),
        scratch_shapes=[pltpu.VMEM((heads, 1), F32), pltpu.VMEM((heads, 1), F32),
                        pltpu.VMEM((heads, dh), F32), pltpu.VMEM((heads, 1), F32)])
    return pl.pallas_call(
        functools.partial(_fox_dec_kernel, g_pages=g_pages, scale=dh ** -0.5),
        grid_spec=grid_spec,
        out_shape=jax.ShapeDtypeStruct((bsz, heads, dh), BF16),
        compiler_params=_cparams(("parallel", "arbitrary")),
        name="fox_decode",
    )(page_table, q, kn, vn, lfn, *([lf_cache] * g_pages), *([k_cache] * g_pages), *([v_cache] * g_pages))


def _sb_decode(page_table, q, k_cache, v_cache, layer, g_pages=16):
    bsz, n_pages = page_table.shape
    _, _, page, heads, dh = k_cache.shape
    g_pages = _tile(n_pages, g_pages)
    grid_spec = pltpu.PrefetchScalarGridSpec(
        num_scalar_prefetch=1,
        grid=(bsz, n_pages // g_pages),
        in_specs=([_per_seq_spec((heads, dh))]
                  + _page_specs(n_pages, g_pages, (page, heads, dh), layer)
                  + _page_specs(n_pages, g_pages, (page, heads, dh), layer)),
        out_specs=_per_seq_spec((heads, dh)),
        scratch_shapes=[pltpu.VMEM((heads, dh), F32), pltpu.VMEM((heads, 1), F32)])
    return pl.pallas_call(
        functools.partial(_sb_dec_kernel, g_pages=g_pages, scale=dh ** -0.5),
        grid_spec=grid_spec,
        out_shape=jax.ShapeDtypeStruct((bsz, heads, dh), BF16),
        compiler_params=_cparams(("parallel", "arbitrary")),
        name="sb_decode",
    )(page_table, q, *([k_cache] * g_pages), *([v_cache] * g_pages))


def _diff_decode(page_table, q, kn, vn, slopes, lq1, lk1, lq2, lk2, g, k_cache, v_cache, layer, lambda_init,
                 g_pages=8):
    bsz, n_pages = page_table.shape
    _, _, page, heads, dv = k_cache.shape
    dq = dv // 2
    rows = 2 * heads
    g_pages = _tile(n_pages, g_pages)
    grid_spec = pltpu.PrefetchScalarGridSpec(
        num_scalar_prefetch=1,
        grid=(bsz, n_pages // g_pages),
        in_specs=([_per_seq_spec((rows, dv)), _per_seq_spec((heads, dv)), _per_seq_spec((heads, dv)),
                   _shared_spec((rows, 1)),
                   _shared_spec((1, dq)), _shared_spec((1, dq)), _shared_spec((1, dq)), _shared_spec((1, dq)),
                   _shared_spec((1, dv))]
                  + _page_specs(n_pages, g_pages, (page, heads, dv), layer)
                  + _page_specs(n_pages, g_pages, (page, heads, dv), layer)),
        out_specs=_per_seq_spec((heads, dv)),
        scratch_shapes=[pltpu.VMEM((rows, 1), F32), pltpu.VMEM((rows, 1), F32),
                        pltpu.VMEM((rows, dv), F32)])
    return pl.pallas_call(
        functools.partial(_diff_dec_kernel, g_pages=g_pages, scale=dq ** -0.5, lambda_init=lambda_init,
                          n_pages=n_pages),
        grid_spec=grid_spec,
        out_shape=jax.ShapeDtypeStruct((bsz, heads, dv), BF16),
        compiler_params=_cparams(("parallel", "arbitrary")),
        name="diff_decode",
    )(page_table, q, kn, vn, slopes, lq1, lk1, lq2, lk2, g, *([k_cache] * g_pages), *([v_cache] * g_pages))


def _even_layer(xp, xpb, xsb, caches, page_table, w_in, b_f, w_o, dims):
    cache_k_a, cache_v_a, cache_lf, cache_k_b, cache_v_b, layer_idx = caches
    bsz, t, dbs, heads, dh = dims
    w = heads * dh
    scale = dh ** -0.5
    wb = w_in[:, :3 * w].astype(BF16)
    wb_b = w_in[:, 3 * w + heads:].astype(BF16)
    w_f = jnp.pad(w_in[:, 3 * w:3 * w + heads].astype(BF16), ((0, 0), (0, LANES - heads)))
    b_pad = jnp.pad(b_f.astype(F32), (0, LANES - heads)).reshape(1, LANES)
    wo_a, wo_b = w_o[:w].astype(BF16), w_o[w:].astype(BF16)

    if xpb is None:
        q_a, xpb, lf = _entry_projection(xp, wb, w, scale * LOG2E, w_f, b_pad)
    else:
        (q_a,) = _mm(xpb, wb, (BF16,), out_scale=scale * LOG2E, col0=0, n=w)
        lf = _forget_gate(xpb, w_f, b_pad)
    lf = lf[:, :heads]
    k_a, k_ab = _mm(xpb, wb, (F32, BF16), col0=w, n=w)
    v_a, v_ab = _mm(xpb, wb, (F32, BF16), col0=2 * w, n=w)
    (q_b,) = _mm(xpb, wb_b, (BF16,), out_scale=scale * LOG2E, col0=0, n=w)
    k_b, k_bb = _mm(xpb, wb_b, (F32, BF16), col0=w, n=w)
    v_b, v_bb = _mm(xpb, wb_b, (F32, BF16), col0=2 * w, n=w)
    lf_t = jnp.swapaxes(lf.reshape(bsz, t, heads), 1, 2)
    cum = _suffix_sums(lf_t.reshape(bsz * heads, t), LOG2E).reshape(bsz, heads, 1, t)
    o_a = _fox_attention(q_a, k_ab, v_ab, cum, bsz, heads, dh)
    o_b = _sb_attention(q_b, k_bb, v_bb, bsz, heads, dh)
    new_p = (k_a.reshape(bsz, t, heads, dh), v_a.reshape(bsz, t, heads, dh), lf.reshape(bsz, t, heads),
             k_b.reshape(bsz, t, heads, dh), v_b.reshape(bsz, t, heads, dh))

    (qkv_a,) = _mm(xsb, wb, (F32,), col0=0, n=3 * w)
    (qkv_b,) = _mm(xsb, wb_b, (F32,), col0=0, n=3 * w)
    sq_a, sk_a, sv_a = (qkv_a[:, i * w:(i + 1) * w] for i in range(3))
    sq_b, sk_b, sv_b = (qkv_b[:, i * w:(i + 1) * w] for i in range(3))
    slf = _forget_gate(xsb, w_f, b_pad)[:, :heads]
    per_head = lambda a: a.reshape(dbs, heads, dh)
    n_l, n_pool, page, _ = cache_lf.shape
    lf_flat = cache_lf.reshape(n_l, n_pool, 1, page * heads)
    so_a = _fox_decode(page_table, per_head(sq_a), per_head(sk_a), per_head(sv_a), slf.reshape(dbs, heads, 1),
                       lf_flat, cache_k_a, cache_v_a, layer_idx)
    so_b = _sb_decode(page_table, per_head(sq_b), cache_k_b, cache_v_b, layer_idx)
    new_s = (sk_a.reshape(dbs, 1, heads, dh), sv_a.reshape(dbs, 1, heads, dh), slf.reshape(dbs, 1, heads),
             sk_b.reshape(dbs, 1, heads, dh), sv_b.reshape(dbs, 1, heads, dh))
    return (o_a, o_b), (so_a.reshape(dbs, w), so_b.reshape(dbs, w)), (wo_a, wo_b), new_p, new_s


def _odd_layer(xpb, xsb, caches, page_table, w_in, lq1, lk1, lq2, lk2, gain, w_o, lambda_init, dims):
    cache_k, cache_v, layer_idx = caches
    bsz, t, dbs, heads, dq = dims
    dv = 2 * dq
    w = heads * dv
    scale = dq ** -0.5
    wb = w_in.astype(BF16)
    slopes = 2.0 ** (-8.0 * jnp.arange(1, heads + 1, dtype=F32) / heads)
    vecs = [a.astype(F32).reshape(1, dq) for a in (lq1, lk1, lq2, lk2)]
    g_row = gain.astype(F32).reshape(1, dv)

    (q,) = _mm(xpb, wb, (BF16,), out_scale=scale * LOG2E, col0=0, n=w)
    k, kb = _mm(xpb, wb, (F32, BF16), col0=w, n=w)
    v, vb = _mm(xpb, wb, (F32, BF16), col0=2 * w, n=w)
    o = _diff_attention(q, kb, vb, slopes * LOG2E, *vecs, g_row, bsz, heads, dq, lambda_init)
    new_p = (k.reshape(bsz, t, heads, dv), v.reshape(bsz, t, heads, dv))

    (qkv,) = _mm(xsb, wb, (F32,), col0=0, n=3 * w)
    sq, sk, sv = (qkv[:, i * w:(i + 1) * w] for i in range(3))
    sq4 = sq.reshape(dbs, heads, 2, dq)
    zero = jnp.zeros((dbs, heads, dq), F32)
    q_rows = jnp.concatenate([jnp.concatenate([sq4[:, :, 0], zero], axis=2),
                              jnp.concatenate([zero, sq4[:, :, 1]], axis=2)], axis=1)
    slope_rows = jnp.concatenate([slopes, slopes]).reshape(2 * heads, 1)
    so = _diff_decode(page_table, q_rows, sk.reshape(dbs, heads, dv), sv.reshape(dbs, heads, dv), slope_rows,
                      *vecs, g_row, cache_k, cache_v, layer_idx, lambda_init)
    new_s = (sk.reshape(dbs, 1, heads, dv), sv.reshape(dbs, 1, heads, dv))
    return (o,), (so.reshape(dbs, w),), (w_o.astype(BF16),), new_p, new_s


def kernel(x_prompt, x_sample, cache_fox_k, cache_fox_v, cache_fox_logf, cache_sb_k, cache_sb_v, cache_diff_k, cache_diff_v, page_table, w_in_even, b_forget, w_out_even, w_in_odd, lambda_q1, lambda_k1, lambda_q2, lambda_k2, subln_gain, w_out_odd, ln_mix_gain, ln_mix_bias, ln_ffn_gain, ln_ffn_bias, w_ffn_up, w_ffn_down):
    bsz, t, d = x_prompt.shape
    dbs, ts, _ = x_sample.shape
    assert ts == 1, "the sample group carries one new token per sequence"
    depth = ln_mix_gain.shape[0]
    alpha = (2 * depth) ** 0.25
    h_fox, dh = cache_fox_k.shape[3], cache_fox_k.shape[4]
    h_diff, dq = cache_diff_k.shape[3], cache_diff_k.shape[4] // 2
    assert cache_sb_k.shape[3:] == (h_fox, dh), "both head groups of an even layer share one geometry"

    xp = x_prompt.reshape(bsz * t, d)
    xs = x_sample.reshape(dbs, d)
    xpb, xsb = None, xs.astype(BF16)
    even_p, even_s, odd_p, odd_s = [], [], [], []
    row = lambda a: a.astype(F32).reshape(1, d)
    for layer in range(depth):
        i = layer // 2
        if layer % 2 == 0:
            caches = (cache_fox_k, cache_fox_v, cache_fox_logf, cache_sb_k, cache_sb_v, i)
            o_p, o_s, w_o, new_p, new_s = _even_layer(xp, xpb, xsb, caches, page_table, w_in_even[i],
                                                      b_forget[i], w_out_even[i], (bsz, t, dbs, h_fox, dh))
            even_p.append(new_p)
            even_s.append(new_s)
        else:
            lambda_init = 0.8 - 0.6 * math.exp(-0.3 * layer)
            caches = (cache_diff_k, cache_diff_v, i)
            o_p, o_s, w_o, new_p, new_s = _odd_layer(xpb, xsb, caches, page_table, w_in_odd[i], lambda_q1[i],
                                                     lambda_k1[i], lambda_q2[i], lambda_k2[i], subln_gain[i],
                                                     w_out_odd[i], lambda_init, (bsz, t, dbs, h_diff, dq))
            odd_p.append(new_p)
            odd_s.append(new_s)
        g1, b1 = row(ln_mix_gain[layer]), row(ln_mix_bias[layer])
        g2, b2 = row(ln_ffn_gain[layer]), row(ln_ffn_bias[layer])
        xp, xpb = _proj_residual_ln(list(o_p), list(w_o), xp, g1, b1, alpha)
        xs, xsb = _proj_residual_ln(list(o_s), list(w_o), xs, g1, b1, alpha)
        xs, xsb, w_up, w_down = _ffn_residual_ln_casting(xsb, xs, w_ffn_up, w_ffn_down, layer, g2, b2, alpha)
        xp, xpb = _ffn_residual_ln(xpb, xp, w_up, w_down, g2, b2, alpha)

    def stk(rows, j):
        return jnp.stack([r[j] for r in rows])

    return (xp.reshape(bsz, t, d), xs.reshape(dbs, 1, d),
            stk(even_p, 0), stk(even_p, 1), stk(even_p, 2), stk(even_p, 3), stk(even_p, 4), stk(odd_p, 0), stk(odd_p, 1),
            stk(even_s, 0), stk(even_s, 1), stk(even_s, 2), stk(even_s, 3), stk(even_s, 4), stk(odd_s, 0), stk(odd_s, 1))
```
